```python
import math
import jax, jax.numpy as jnp
from jax import lax
import numpy as np

D_MODEL = 1024
BATCH = 32
SEQ = 2048
DEPTH = 1

CHUNK = 64
QBLOCK = 128
HEAD_DIM = 64
MIX_WIDTH = D_MODEL
A_HEADS = 4
A_WIDTH = A_HEADS * 2 * HEAD_DIM
B_HEADS = 8
B_WIDTH = B_HEADS * HEAD_DIM
IN_WIDTH = 6 * 512
D_FF = ((8 * D_MODEL // 3 + 255) // 256) * 256
PLE_DIM = 256
ROPE_THETA = 10000.0
LN_EPS = 1e-5
DEEPNORM_ALPHA = (2.0 * DEPTH) ** 0.25
DEEPNORM_BETA = (8.0 * DEPTH) ** -0.25

kernel_name = "hymba_diff_stickbreak_deepnorm_encoder"


def layer_norm(x, g, b):
    xf = x.astype(jnp.float32)
    mu = jnp.mean(xf, axis=-1, keepdims=True)
    var = jnp.mean(jnp.square(xf - mu), axis=-1, keepdims=True)
    y = (xf - mu) * lax.rsqrt(var + LN_EPS)
    return (y * g.astype(jnp.float32) + b.astype(jnp.float32)).astype(x.dtype)


def rms_norm(x, g):
    xf = x.astype(jnp.float32)
    y = xf * lax.rsqrt(jnp.mean(jnp.square(xf), axis=-1, keepdims=True) + LN_EPS)
    return y * g.astype(jnp.float32)


def rope_tables(seq_len):
    inv_freq = 1.0 / (ROPE_THETA ** (jnp.arange(0, HEAD_DIM, 2, dtype=jnp.float32) / HEAD_DIM))
    ang = jnp.arange(seq_len, dtype=jnp.float32)[:, None] * inv_freq[None, :]
    ang = jnp.concatenate([ang, ang], axis=-1)
    return jnp.cos(ang), jnp.sin(ang)


def apply_rope(t, cos, sin):
    tf = t.astype(jnp.float32)
    t1, t2 = jnp.split(tf, 2, axis=-1)
    rot = jnp.concatenate([-t2, t1], axis=-1)
    return tf * cos[None, :, None, :] + rot * sin[None, :, None, :]


def differential_attention(q, k, v, lam):
    S = q.shape[1]
    scale = 1.0 / math.sqrt(HEAD_DIM)
    vf = v.astype(jnp.float32)
    outs = []
    for blk in range(S // QBLOCK):
        t0, t1 = blk * QBLOCK, (blk + 1) * QBLOCK
        qpos = jnp.arange(t0, t1)
        kpos = jnp.arange(t1)
        mask = (qpos[:, None] // CHUNK) >= (kpos[None, :] // CHUNK)
        s = jnp.einsum('bqhmd,bkhmd->bhmqk', q[:, t0:t1], k[:, :t1]) * scale
        s = jnp.where(mask[None, None, None], s, -jnp.inf)
        pr = jax.nn.softmax(s, axis=-1)
        w = pr[:, :, 0] - lam * pr[:, :, 1]
        outs.append(jnp.einsum('bhqk,bkhe->bqhe', w, vf[:, :t1]))
    return jnp.concatenate(outs, axis=1)


def stick_breaking_attention(q, k, v):
    S = q.shape[1]
    scale = 1.0 / math.sqrt(HEAD_DIM)
    qf, kf, vf = q.astype(jnp.float32), k.astype(jnp.float32), v.astype(jnp.float32)
    outs = []
    for blk in range(S // QBLOCK):
        t0, t1 = blk * QBLOCK, (blk + 1) * QBLOCK
        qpos = jnp.arange(t0, t1)
        kpos = jnp.arange(t1)
        mask = (kpos[None, :] < qpos[:, None])[None, None]
        z = jnp.einsum('bqhd,bkhd->bhqk', qf[:, t0:t1], kf[:, :t1]) * scale
        log_beta = jax.nn.log_sigmoid(z)
        log_1m_beta = jnp.where(mask, -jax.nn.softplus(z), 0.0)
        suffix = lax.cumsum(log_1m_beta, axis=3, reverse=True) - log_1m_beta
        a = jnp.where(mask, jnp.exp(log_beta + suffix), 0.0)
        outs.append(jnp.einsum('bhqk,bkhd->bqhd', a, vf[:, :t1]))
    return jnp.concatenate(outs, axis=1)


def setup_inputs(seed: int = 0) -> dict:
    key = jax.random.key(seed)
    ks = jax.random.split(key, 24)
    f32 = jnp.float32
    nrm = lambda k, shape, s: jax.random.normal(k, shape, f32) * s
    gain = lambda k, shape: 1.0 + 0.02 * jax.random.normal(k, shape, f32)
    L, D = DEPTH, D_MODEL
    return {
        "x": jax.random.normal(ks[0], (BATCH, SEQ, D), f32),
        "p": jax.random.normal(ks[1], (DEPTH, BATCH, SEQ, PLE_DIM), f32),
        "ln_emb_g": gain(ks[2], (D,)),
        "ln_emb_b": nrm(ks[3], (D,), 0.02),
        "w_in": nrm(ks[4], (L, D, IN_WIDTH), D ** -0.5),
        "lam_q1": nrm(ks[5], (L, HEAD_DIM), 0.1),
        "lam_k1": nrm(ks[6], (L, HEAD_DIM), 0.1),
        "lam_q2": nrm(ks[7], (L, HEAD_DIM), 0.1),
        "lam_k2": nrm(ks[8], (L, HEAD_DIM), 0.1),
        "subln_g": gain(ks[9], (L, 2 * HEAD_DIM)),
        "w_out": nrm(ks[10], (L, MIX_WIDTH, D), DEEPNORM_BETA * MIX_WIDTH ** -0.5),
        "ln1_g": gain(ks[11], (L, D)),
        "ln1_b": nrm(ks[12], (L, D), 0.02),
        "w_ffn_gate": nrm(ks[13], (L, D, D_FF), D ** -0.5),
        "w_ffn_up": nrm(ks[14], (L, D, D_FF), D ** -0.5),
        "w_ffn_down": nrm(ks[15], (L, D_FF, D), DEEPNORM_BETA * D_FF ** -0.5),
        "ln2_g": gain(ks[16], (L, D)),
        "ln2_b": nrm(ks[17], (L, D), 0.02),
        "w_ple_gate": nrm(ks[18], (L, D, D), D ** -0.5),
        "b_ple_gate": nrm(ks[19], (L, D), 0.02),
        "w_ple_proj": nrm(ks[20], (L, PLE_DIM, D), DEEPNORM_BETA * PLE_DIM ** -0.5),
        "ln3_g": gain(ks[21], (L, D)),
        "ln3_b": nrm(ks[22], (L, D), 0.02),
    }


def reference(x, p, ln_emb_g, ln_emb_b, w_in, lam_q1, lam_k1, lam_q2, lam_k2, subln_g, w_out,
              ln1_g, ln1_b, w_ffn_gate, w_ffn_up, w_ffn_down, ln2_g, ln2_b,
              w_ple_gate, b_ple_gate, w_ple_proj, ln3_g, ln3_b):
    B, S, _ = x.shape
    cos, sin = rope_tables(S)
    h = layer_norm(x, ln_emb_g, ln_emb_b)
    for i in range(DEPTH):
        lambda_init = 0.8 - 0.6 * math.exp(-0.3 * i)
        proj = h @ w_in[i]
        aq, ak, av, bq, bk, bv = jnp.split(proj, 6, axis=-1)
        aq = apply_rope(aq.reshape(B, S, 2 * A_HEADS, HEAD_DIM), cos, sin).reshape(B, S, A_HEADS, 2, HEAD_DIM)
        ak = apply_rope(ak.reshape(B, S, 2 * A_HEADS, HEAD_DIM), cos, sin).reshape(B, S, A_HEADS, 2, HEAD_DIM)
        av = av.reshape(B, S, A_HEADS, 2 * HEAD_DIM)
        lam = (jnp.exp(jnp.sum(lam_q1[i].astype(jnp.float32) * lam_k1[i].astype(jnp.float32)))
               - jnp.exp(jnp.sum(lam_q2[i].astype(jnp.float32) * lam_k2[i].astype(jnp.float32)))
               + lambda_init)
        oa = differential_attention(aq, ak, av, lam)
        oa = rms_norm(oa, subln_g[i]) * (1.0 - lambda_init)
        ob = stick_breaking_attention(bq.reshape(B, S, B_HEADS, HEAD_DIM),
                                      bk.reshape(B, S, B_HEADS, HEAD_DIM),
                                      bv.reshape(B, S, B_HEADS, HEAD_DIM))
        mix = jnp.concatenate([oa.reshape(B, S, A_WIDTH), ob.reshape(B, S, B_WIDTH)], axis=-1).astype(h.dtype)
        h = layer_norm(DEEPNORM_ALPHA * h + mix @ w_out[i], ln1_g[i], ln1_b[i])
        f = (jax.nn.silu(h @ w_ffn_gate[i]) * (h @ w_ffn_up[i])) @ w_ffn_down[i]
        h = layer_norm(DEEPNORM_ALPHA * h + f, ln2_g[i], ln2_b[i])
        gate = jax.nn.sigmoid(h @ w_ple_gate[i] + b_ple_gate[i])
        e = p[i] @ w_ple_proj[i]
        h = layer_norm(DEEPNORM_ALPHA * h + gate * e, ln3_g[i], ln3_b[i])
    return h
```

```python
import functools
import math

import jax
import jax.numpy as jnp
from jax import lax
from jax.experimental import pallas as pl
from jax.experimental.pallas import tpu as pltpu

D_MODEL = 1024
SEQ = 2048
DEPTH = 1
CHUNK = 64
HEAD_DIM = 64
A_HEADS = 4
B_HEADS = 8
SECTION = 512
IN_WIDTH = 6 * SECTION
D_FF = 2816
PLE_DIM = 256
ROPE_THETA = 10000.0
LN_EPS = 1e-5
DEEPNORM_ALPHA = (2.0 * DEPTH) ** 0.25
LAMBDA_INIT = 0.8 - 0.6 * math.exp(-0.3 * 0)
QK_SCALE = 1.0 / math.sqrt(HEAD_DIM)

LANES = 128
VMEM_LIMIT = 56 * 1024 * 1024

TM_PROJ = 512
TM_POST = 512
FF_CHUNK = 256
TQ = 256
TK = 256

_NT = (((1,), (1,)), ((), ()))


def _ln(x, g, b):
    mu = jnp.mean(x, axis=-1, keepdims=True)
    xc = x - mu
    var = jnp.mean(xc * xc, axis=-1, keepdims=True)
    return xc * lax.rsqrt(var + LN_EPS) * g + b


def _resident(shape):
    return pl.BlockSpec(shape, lambda *_: (0,) * len(shape), pipeline_mode=pl.Buffered(1))


def _in_proj_kernel(x_ref, g_ref, b_ref, w_ref, rope_ref, o_ref):
    hb = _ln(x_ref[...], g_ref[...], b_ref[...]).astype(jnp.bfloat16)
    for sec in range(6):
        acc = jnp.dot(hb, w_ref[:, sec * SECTION:(sec + 1) * SECTION],
                      preferred_element_type=jnp.float32)
        if sec < 2:
            cos = rope_ref[3 * sec + 0]
            sin_lo = rope_ref[3 * sec + 1]
            sin_hi = rope_ref[3 * sec + 2]
            for c in range(SECTION // LANES):
                t = acc[:, c * LANES:(c + 1) * LANES]
                r = (t * cos + pltpu.roll(t, LANES - HEAD_DIM // 2, 1) * sin_lo
                     + pltpu.roll(t, HEAD_DIM // 2, 1) * sin_hi)
                o_ref[:, sec * SECTION + c * LANES: sec * SECTION + (c + 1) * LANES] = r.astype(o_ref.dtype)
        else:
            if sec == 3:
                acc = acc * QK_SCALE
            o_ref[:, sec * SECTION:(sec + 1) * SECTION] = acc.astype(o_ref.dtype)


def _rope_tables():
    half = HEAD_DIM // 2
    inv_freq = 1.0 / (ROPE_THETA ** (jnp.arange(0, HEAD_DIM, 2, dtype=jnp.float32) / HEAD_DIM))
    ang = jnp.arange(SEQ, dtype=jnp.float32)[:, None] * inv_freq[None, :]
    ang = jnp.concatenate([ang, ang], axis=-1)
    cos, sin = jnp.cos(ang), jnp.sin(ang)
    first = (jnp.arange(HEAD_DIM) < half)[None, :]
    sin_lo = jnp.where(first, -sin, 0.0)
    sin_hi = jnp.where(first, 0.0, sin)
    k_tabs = jnp.stack([jnp.tile(t, (1, LANES // HEAD_DIM)) for t in (cos, sin_lo, sin_hi)])
    return jnp.concatenate([k_tabs * QK_SCALE, k_tabs], axis=0)


def _in_proj(x2d, g, b, w_bf16, rope):
    n = x2d.shape[0]
    seq_tiles = SEQ // TM_PROJ
    return pl.pallas_call(
        _in_proj_kernel,
        out_shape=jax.ShapeDtypeStruct((n, IN_WIDTH), jnp.bfloat16),
        grid=(n // TM_PROJ,),
        in_specs=[
            pl.BlockSpec((TM_PROJ, D_MODEL), lambda i: (i, 0)),
            _resident((1, D_MODEL)),
            _resident((1, D_MODEL)),
            _resident((D_MODEL, IN_WIDTH)),
            pl.BlockSpec((6, TM_PROJ, LANES), lambda i: (0, i % seq_tiles, 0)),
        ],
        out_specs=pl.BlockSpec((TM_PROJ, IN_WIDTH), lambda i: (i, 0)),
        compiler_params=pltpu.CompilerParams(
            dimension_semantics=("parallel",), vmem_limit_bytes=VMEM_LIMIT),
        name="in_proj",
    )(x2d, g, b, w_bf16, rope)


def _attn_diff_kernel(lam_ref, q_ref, k_ref, v_ref, g_ref, o_ref):
    qi = pl.program_id(2)
    lp = lam_ref[...]
    lam = (jnp.exp(jnp.sum(lp[0:1] * lp[1:2], axis=-1, keepdims=True))
           - jnp.exp(jnp.sum(lp[2:3] * lp[3:4], axis=-1, keepdims=True)) + LAMBDA_INIT)

    q = q_ref[...]
    lane = lax.broadcasted_iota(jnp.int32, (1, LANES), 1)
    zero = jnp.zeros_like(q)
    qs = (jnp.where(lane < HEAD_DIM, q, zero), jnp.where(lane >= HEAD_DIM, q, zero))

    def step(kb, carry, masked):
        k = k_ref[pl.ds(pl.multiple_of(kb * TK, TK), TK), :]
        v = v_ref[pl.ds(pl.multiple_of(kb * TK, TK), TK), :]
        if masked:
            qc = lax.broadcasted_iota(jnp.int32, (TQ, TK), 0) // CHUNK
            kc = lax.broadcasted_iota(jnp.int32, (TQ, TK), 1) // CHUNK
            visible = qc >= kc
        out = []
        for m in range(2):
            m_old, l_old, acc = carry[m]
            s = lax.dot_general(qs[m], k, _NT, preferred_element_type=jnp.float32)
            if masked:
                s = jnp.where(visible, s, -jnp.inf)
            m_new = jnp.maximum(m_old, jnp.max(s, axis=-1, keepdims=True))
            alpha = jnp.exp(m_old - m_new)
            p = jnp.exp(s - m_new)
            l_new = alpha * l_old + jnp.sum(p, axis=-1, keepdims=True)
            acc = alpha * acc + jnp.dot(p.astype(jnp.bfloat16), v, preferred_element_type=jnp.float32)
            out.append((m_new, l_new, acc))
        return tuple(out)

    init = tuple((jnp.full((TQ, 1), -jnp.inf, jnp.float32), jnp.zeros((TQ, 1), jnp.float32),
                  jnp.zeros((TQ, LANES), jnp.float32)) for _ in range(2))
    carry = lax.fori_loop(0, qi, lambda kb, c: step(kb, c, False), init)
    (_, l1, a1), (_, l2, a2) = step(qi, carry, True)

    o = a1 / l1 - lam * (a2 / l2)
    y = o * lax.rsqrt(jnp.mean(o * o, axis=-1, keepdims=True) + LN_EPS) * g_ref[...]
    o_ref[...] = (y * (1.0 - LAMBDA_INIT)).astype(o_ref.dtype)


def _attn_diff(proj, lam_params, subln_g):
    bsz = proj.shape[0]
    sec_blocks = SECTION // LANES
    return pl.pallas_call(
        _attn_diff_kernel,
        out_shape=jax.ShapeDtypeStruct((bsz, SEQ, A_HEADS * LANES), jnp.bfloat16),
        grid=(bsz, A_HEADS, SEQ // TQ),
        in_specs=[
            _resident((4, HEAD_DIM)),
            pl.BlockSpec((None, TQ, LANES), lambda b, h, i: (b, i, h)),
            pl.BlockSpec((None, SEQ, LANES), lambda b, h, i: (b, 0, sec_blocks + h)),
            pl.BlockSpec((None, SEQ, LANES), lambda b, h, i: (b, 0, 2 * sec_blocks + h)),
            _resident((1, LANES)),
        ],
        out_specs=pl.BlockSpec((None, TQ, LANES), lambda b, h, i: (b, i, h)),
        compiler_params=pltpu.CompilerParams(
            dimension_semantics=("parallel", "parallel", "arbitrary"), vmem_limit_bytes=VMEM_LIMIT),
        name="attn_diff",
    )(lam_params, proj, proj, proj, subln_g)


def _attn_stick_kernel(q_ref, k_ref, v_ref, o_ref):
    qi = pl.program_id(2)
    q = q_ref[...]
    lane = lax.broadcasted_iota(jnp.int32, (1, LANES), 1)
    zero = jnp.zeros_like(q)
    qs = (jnp.where(lane < HEAD_DIM, q, zero), jnp.where(lane >= HEAD_DIM, q, zero))
    later = (lax.broadcasted_iota(jnp.int32, (TK, TK), 0)
             > lax.broadcasted_iota(jnp.int32, (TK, TK), 1)).astype(jnp.bfloat16)

    def step(kb, carry, masked):
        k = k_ref[pl.ds(pl.multiple_of(kb * TK, TK), TK), :]
        v = v_ref[pl.ds(pl.multiple_of(kb * TK, TK), TK), :]
        if masked:
            causal = (lax.broadcasted_iota(jnp.int32, (TQ, TK), 1)
                      < lax.broadcasted_iota(jnp.int32, (TQ, TK), 0))
        out = []
        for h in range(2):
            c, acc = carry[h]
            z = lax.dot_general(qs[h], k, _NT, preferred_element_type=jnp.float32)
            sp = jnp.maximum(z, 0.0) + jnp.log(1.0 + jnp.exp(-jnp.abs(z)))
            log_beta = z - sp
            log_surv = -sp
            if masked:
                log_surv = jnp.where(causal, log_surv, 0.0)
            hi = log_surv.astype(jnp.bfloat16)
            lo = (log_surv - hi.astype(jnp.float32)).astype(jnp.bfloat16)
            suffix = (jnp.dot(hi, later, preferred_element_type=jnp.float32)
                      + jnp.dot(lo, later, preferred_element_type=jnp.float32))
            a = jnp.exp(log_beta + suffix + c)
            if masked:
                a = jnp.where(causal, a, 0.0)
            acc = acc + jnp.dot(a.astype(jnp.bfloat16), v, preferred_element_type=jnp.float32)
            c = c + jnp.sum(log_surv, axis=-1, keepdims=True)
            out.append((c, acc))
        return tuple(out)

    init = tuple((jnp.zeros((TQ, 1), jnp.float32), jnp.zeros((TQ, LANES), jnp.float32)) for _ in range(2))
    carry = step(qi, init, True)
    carry = lax.fori_loop(0, qi, lambda i, c: step(qi - 1 - i, c, False), carry)
    o_ref[...] = jnp.where(lane < HEAD_DIM, carry[0][1], carry[1][1]).astype(o_ref.dtype)


def _attn_stick(proj):
    bsz = proj.shape[0]
    sec_blocks = SECTION // LANES
    return pl.pallas_call(
        _attn_stick_kernel,
        out_shape=jax.ShapeDtypeStruct((bsz, SEQ, SECTION), jnp.bfloat16),
        grid=(bsz, sec_blocks, SEQ // TQ),
        in_specs=[
            pl.BlockSpec((None, TQ, LANES), lambda b, h, i: (b, i, 3 * sec_blocks + h)),
            pl.BlockSpec((None, SEQ, LANES), lambda b, h, i: (b, 0, 4 * sec_blocks + h)),
            pl.BlockSpec((None, SEQ, LANES), lambda b, h, i: (b, 0, 5 * sec_blocks + h)),
        ],
        out_specs=pl.BlockSpec((None, TQ, LANES), lambda b, h, i: (b, i, h)),
        compiler_params=pltpu.CompilerParams(
            dimension_semantics=("parallel", "parallel", "arbitrary"), vmem_limit_bytes=VMEM_LIMIT),
        name="attn_stick",
    )(proj, proj, proj)


def _post_kernel(x_ref, ma_ref, mb_ref, p_ref, ge_ref, be_ref, wo_ref, g1_ref, b1_ref,
                 wg_ref, wu_ref, wd_ref, g2_ref, b2_ref, wpg_ref, bpg_ref, wpp_ref, g3_ref, b3_ref, o_ref):
    f32, bf16 = jnp.float32, jnp.bfloat16
    h = _ln(x_ref[...], ge_ref[...], be_ref[...])
    y = (jnp.dot(ma_ref[...], wo_ref[0:SECTION, :], preferred_element_type=f32)
         + jnp.dot(mb_ref[...], wo_ref[SECTION:2 * SECTION, :], preferred_element_type=f32))
    h = _ln(DEEPNORM_ALPHA * h + y, g1_ref[...], b1_ref[...])

    hb = h.astype(bf16)
    f = jnp.zeros(h.shape, f32)
    for c in range(D_FF // FF_CHUNK):
        cols = slice(c * FF_CHUNK, (c + 1) * FF_CHUNK)
        gate = jnp.dot(hb, wg_ref[:, cols], preferred_element_type=f32)
        up = jnp.dot(hb, wu_ref[:, cols], preferred_element_type=f32)
        act = (gate * jax.nn.sigmoid(gate) * up).astype(bf16)
        f = f + jnp.dot(act, wd_ref[cols, :], preferred_element_type=f32)
    h = _ln(DEEPNORM_ALPHA * h + f, g2_ref[...], b2_ref[...])

    gate = jax.nn.sigmoid(jnp.dot(h.astype(bf16), wpg_ref[...], preferred_element_type=f32) + bpg_ref[...])
    e = jnp.dot(p_ref[...].astype(bf16), wpp_ref[...], preferred_element_type=f32)
    o_ref[...] = _ln(DEEPNORM_ALPHA * h + gate * e, g3_ref[...], b3_ref[...])


def _post(x2d, mix_a, mix_b, p2d, vecs, mats):
    n = x2d.shape[0]
    ge, be, g1, b1, g2, b2, bpg, g3, b3 = vecs
    wo, wg, wu, wd, wpg, wpp = mats
    row = lambda w: pl.BlockSpec((TM_POST, w), lambda i: (i, 0))
    vec = _resident((1, D_MODEL))
    return pl.pallas_call(
        _post_kernel,
        out_shape=jax.ShapeDtypeStruct((n, D_MODEL), jnp.float32),
        grid=(n // TM_POST,),
        in_specs=[row(D_MODEL), row(SECTION), row(SECTION), row(PLE_DIM), vec, vec,
                  _resident(wo.shape), vec, vec,
                  _resident(wg.shape), _resident(wu.shape), _resident(wd.shape), vec, vec,
                  _resident(wpg.shape), vec, _resident(wpp.shape), vec, vec],
        out_specs=row(D_MODEL),
        compiler_params=pltpu.CompilerParams(
            dimension_semantics=("parallel",), vmem_limit_bytes=VMEM_LIMIT),
        name="post",
    )(x2d, mix_a, mix_b, p2d, ge, be, wo, g1, b1, wg, wu, wd, g2, b2, wpg, bpg, wpp, g3, b3)


def kernel(x, p, ln_emb_g, ln_emb_b, w_in, lam_q1, lam_k1, lam_q2, lam_k2, subln_g, w_out, ln1_g, ln1_b,
           w_ffn_gate, w_ffn_up, w_ffn_down, ln2_g, ln2_b, w_ple_gate, b_ple_gate, w_ple_proj, ln3_g, ln3_b):
    bsz, seq, d = x.shape
    assert (seq, d) == (SEQ, D_MODEL) and w_in.shape == (DEPTH, D_MODEL, IN_WIDTH)
    bf16 = jnp.bfloat16
    row = lambda v: v.reshape(1, -1)
    x2d = x.reshape(bsz * seq, d)

    proj = _in_proj(x2d, row(ln_emb_g), row(ln_emb_b), w_in[0].astype(bf16), _rope_tables())
    proj = proj.reshape(bsz, seq, IN_WIDTH)
    lam_params = jnp.stack([lam_q1[0], lam_k1[0], lam_q2[0], lam_k2[0]])
    mix_a = _attn_diff(proj, lam_params, row(subln_g[0]))
    mix_b = _attn_stick(proj)

    vecs = (row(ln_emb_g), row(ln_emb_b), ln1_g, ln1_b, ln2_g, ln2_b, b_ple_gate, ln3_g, ln3_b)
    mats = tuple(w[0].astype(bf16) for w in (w_out, w_ffn_gate, w_ffn_up, w_ffn_down, w_ple_gate, w_ple_proj))
    out = _post(x2d, mix_a.reshape(bsz * seq, SECTION), mix_b.reshape(bsz * seq, SECTION),
                p[0].reshape(bsz * seq, PLE_DIM), vecs, mats)
    return out.reshape(bsz, seq, d)
```

```python
import functools
import math

import jax
import jax.numpy as jnp
from jax import lax
from jax.experimental import pallas as pl
from jax.experimental.pallas import tpu as pltpu

D_MODEL = 1024
SEQ = 2048
DEPTH = 1
CHUNK = 64
HEAD_DIM = 64
A_HEADS = 4
B_HEADS = 8
SECTION = 512
IN_WIDTH = 6 * SECTION
D_FF = 2816
PLE_DIM = 256
ROPE_THETA = 10000.0
LN_EPS = 1e-5
DEEPNORM_ALPHA = (2.0 * DEPTH) ** 0.25
LAMBDA_INIT = 0.8 - 0.6 * math.exp(-0.3 * 0)
Q_SCALE = math.log2(math.e) / math.sqrt(HEAD_DIM)

LANES = 128
VMEM_LIMIT = 56 * 1024 * 1024

TM_PROJ = 512
TM_POST = 512
FF_CHUNK = 256
TQ = 256
TK = 256
NQ = SEQ // TQ

_NT = (((1,), (1,)), ((), ()))


def _ln(x, g, b):
    mu = jnp.mean(x, axis=-1, keepdims=True)
    xc = x - mu
    var = jnp.mean(xc * xc, axis=-1, keepdims=True)
    return xc * lax.rsqrt(var + LN_EPS) * g + b


def _resident(shape):
    return pl.BlockSpec(shape, lambda *_: (0,) * len(shape), pipeline_mode=pl.Buffered(1))


def _split_lanes(q):
    lane = lax.broadcasted_iota(jnp.int32, (1, LANES), 1)
    zero = jnp.zeros_like(q)
    return jnp.where(lane < HEAD_DIM, q, zero), jnp.where(lane >= HEAD_DIM, q, zero)


def _merge(*lists):
    keyed = [((i + 0.5) / len(l), li, i, item) for li, l in enumerate(lists) for i, item in enumerate(l)]
    keyed.sort(key=lambda t: t[:3])
    return [t[3] for t in keyed]


def _in_proj_kernel(x_ref, g_ref, b_ref, w_ref, rope_ref, o_ref):
    hb = _ln(x_ref[...], g_ref[...], b_ref[...]).astype(jnp.bfloat16)
    for sec in range(6):
        acc = jnp.dot(hb, w_ref[:, sec * SECTION:(sec + 1) * SECTION],
                      preferred_element_type=jnp.float32)
        if sec < 2:
            cos = rope_ref[3 * sec + 0]
            sin_lo = rope_ref[3 * sec + 1]
            sin_hi = rope_ref[3 * sec + 2]
            for c in range(SECTION // LANES):
                t = acc[:, c * LANES:(c + 1) * LANES]
                r = (t * cos + pltpu.roll(t, LANES - HEAD_DIM // 2, 1) * sin_lo
                     + pltpu.roll(t, HEAD_DIM // 2, 1) * sin_hi)
                o_ref[:, sec * SECTION + c * LANES: sec * SECTION + (c + 1) * LANES] = r.astype(o_ref.dtype)
        else:
            if sec == 3:
                acc = acc * Q_SCALE
            o_ref[:, sec * SECTION:(sec + 1) * SECTION] = acc.astype(o_ref.dtype)


def _rope_tables():
    half = HEAD_DIM // 2
    inv_freq = 1.0 / (ROPE_THETA ** (jnp.arange(0, HEAD_DIM, 2, dtype=jnp.float32) / HEAD_DIM))
    ang = jnp.arange(SEQ, dtype=jnp.float32)[:, None] * inv_freq[None, :]
    ang = jnp.concatenate([ang, ang], axis=-1)
    cos, sin = jnp.cos(ang), jnp.sin(ang)
    first = (jnp.arange(HEAD_DIM) < half)[None, :]
    sin_lo = jnp.where(first, -sin, 0.0)
    sin_hi = jnp.where(first, 0.0, sin)
    k_tabs = jnp.stack([jnp.tile(t, (1, LANES // HEAD_DIM)) for t in (cos, sin_lo, sin_hi)])
    return jnp.concatenate([k_tabs * Q_SCALE, k_tabs], axis=0)


def _in_proj(x2d, g, b, w_bf16, rope):
    n = x2d.shape[0]
    seq_tiles = SEQ // TM_PROJ
    return pl.pallas_call(
        _in_proj_kernel,
        out_shape=jax.ShapeDtypeStruct((n, IN_WIDTH), jnp.bfloat16),
        grid=(n // TM_PROJ,),
        in_specs=[
            pl.BlockSpec((TM_PROJ, D_MODEL), lambda i: (i, 0)),
            _resident((1, D_MODEL)),
            _resident((1, D_MODEL)),
            _resident((D_MODEL, IN_WIDTH)),
            pl.BlockSpec((6, TM_PROJ, LANES), lambda i: (0, i % seq_tiles, 0)),
        ],
        out_specs=pl.BlockSpec((TM_PROJ, IN_WIDTH), lambda i: (i, 0)),
        compiler_params=pltpu.CompilerParams(
            dimension_semantics=("parallel",), vmem_limit_bytes=VMEM_LIMIT),
        name="in_proj",
    )(x2d, g, b, w_bf16, rope)


def _attn_diff_kernel(lam_ref, q_ref, k_ref, v_ref, g_ref, o_ref):
    f32, bf16 = jnp.float32, jnp.bfloat16
    lp = lam_ref[...]
    lam = (jnp.exp(jnp.sum(lp[0:1] * lp[1:2], axis=-1, keepdims=True))
           - jnp.exp(jnp.sum(lp[2:3] * lp[3:4], axis=-1, keepdims=True)) + LAMBDA_INIT)
    visible = (lax.broadcasted_iota(jnp.int32, (TQ, TK), 0) // CHUNK
               >= lax.broadcasted_iota(jnp.int32, (TQ, TK), 1) // CHUNK)
    maps = range(2)
    qm, s, mx, p, psum, ratio, inv_l1, w = {}, {}, {}, {}, {}, {}, {}, {}

    def score(qi, j):
        if qi not in qm:
            qm[qi] = _split_lanes(q_ref[qi * TQ:(qi + 1) * TQ, :])
        k = k_ref[j * TK:(j + 1) * TK, :]
        for m in maps:
            sj = lax.dot_general(qm[qi][m], k, _NT, preferred_element_type=f32)
            if j == qi:
                sj = jnp.where(visible, sj, -jnp.inf)
            s[qi, j, m] = sj

    def rowmax(qi):
        for m in maps:
            blk = functools.reduce(jnp.maximum, [s[qi, j, m] for j in range(qi + 1)])
            mx[qi, m] = jnp.max(blk, axis=-1, keepdims=True)

    def prob(qi, j):
        for m in maps:
            pj = jnp.exp2(s.pop((qi, j, m)) - mx[qi, m])
            p[qi, j, m] = pj
            psum[qi, m] = pj if j == 0 else psum[qi, m] + pj

    def rowsum(qi):
        l1 = jnp.sum(psum.pop((qi, 0)), axis=-1, keepdims=True)
        l2 = jnp.sum(psum.pop((qi, 1)), axis=-1, keepdims=True)
        inv_l1[qi] = 1.0 / l1
        ratio[qi] = lam * l1 / l2

    def weight(qi, j):
        w[qi, j] = (p.pop((qi, j, 0)) - ratio[qi] * p.pop((qi, j, 1))).astype(bf16)

    def out(qi):
        wp = jnp.concatenate([w.pop((qi, j)) for j in range(qi + 1)], axis=1)
        o = jnp.dot(wp, v_ref[0:(qi + 1) * TK, :], preferred_element_type=f32) * inv_l1.pop(qi)
        y = o * lax.rsqrt(jnp.mean(o * o, axis=-1, keepdims=True) + LN_EPS) * g_ref[...]
        o_ref[qi * TQ:(qi + 1) * TQ, :] = (y * (1.0 - LAMBDA_INIT)).astype(o_ref.dtype)

    for t in range(NQ + 2):
        scoring = [(score, (t, j)) for j in range(t + 1)] if t < NQ else []
        softmax = ([(rowmax, (t - 1,))] + [(prob, (t - 1, j)) for j in range(t)]
                   + [(rowsum, (t - 1,))]) if 1 <= t <= NQ else []
        mixing = ([(weight, (t - 2, j)) for j in range(t - 1)] + [(out, (t - 2,))]) if t >= 2 else []
        for fn, args in _merge(*[l for l in (scoring, softmax, mixing) if l]):
            fn(*args)


def _attn_diff(proj, lam_params, subln_g):
    bsz = proj.shape[0]
    sec_blocks = SECTION // LANES
    return pl.pallas_call(
        _attn_diff_kernel,
        out_shape=jax.ShapeDtypeStruct((bsz, SEQ, A_HEADS * LANES), jnp.bfloat16),
        grid=(bsz, A_HEADS),
        in_specs=[
            _resident((4, HEAD_DIM)),
            pl.BlockSpec((None, SEQ, LANES), lambda b, h: (b, 0, h)),
            pl.BlockSpec((None, SEQ, LANES), lambda b, h: (b, 0, sec_blocks + h)),
            pl.BlockSpec((None, SEQ, LANES), lambda b, h: (b, 0, 2 * sec_blocks + h)),
            _resident((1, LANES)),
        ],
        out_specs=pl.BlockSpec((None, SEQ, LANES), lambda b, h: (b, 0, h)),
        compiler_params=pltpu.CompilerParams(
            dimension_semantics=("parallel", "parallel"), vmem_limit_bytes=VMEM_LIMIT),
        name="attn_diff",
    )(lam_params, proj, proj, proj, subln_g)


def _neg_abs(z):
    bits = lax.bitcast_convert_type(z, jnp.uint32) | jnp.uint32(0x80000000)
    return lax.bitcast_convert_type(bits, jnp.float32)


STICK_SKEW = dict(score=0, logs=1, suffix=1, weight=2, value=2)


def _attn_stick_kernel(q_ref, k_ref, v_ref, o_ref):
    f32, bf16 = jnp.float32, jnp.bfloat16
    lane = lax.broadcasted_iota(jnp.int32, (1, LANES), 1)
    causal = (lax.broadcasted_iota(jnp.int32, (TQ, TK), 1)
              < lax.broadcasted_iota(jnp.int32, (TQ, TK), 0))
    later = (lax.broadcasted_iota(jnp.int32, (TK, TK), 0)
             > lax.broadcasted_iota(jnp.int32, (TK, TK), 1)).astype(bf16)
    later2 = jnp.concatenate([later, later], axis=0)

    units = [(qi, j) for qi in range(NQ) for j in range(qi, -1, -1)]
    st = {u: {} for u in units}
    qs, run_c, acc = {}, {}, {}
    heads = range(2)

    def score(u):
        qi, j = u
        if qi not in qs:
            qs[qi] = _split_lanes(q_ref[qi * TQ:(qi + 1) * TQ, :])
        k = k_ref[j * TK:(j + 1) * TK, :]
        st[u]["z"] = [lax.dot_general(qs[qi][h], k, _NT, preferred_element_type=f32) for h in heads]

    def logs(u):
        qi, j = u
        z = st[u].pop("z")
        lg = [jnp.log2(1.0 + jnp.exp2(_neg_abs(z[h]))) for h in heads]
        lb = [jnp.minimum(z[h], 0.0) - lg[h] for h in heads]
        ls = [lb[h] - z[h] for h in heads]
        if j == qi:
            ls = [jnp.where(causal, ls[h], 0.0) for h in heads]
        hi = [ls[h].astype(bf16) for h in heads]
        lo = [(ls[h] - hi[h].astype(f32)).astype(bf16) for h in heads]
        st[u]["hl"] = [jnp.concatenate([hi[h], lo[h]], axis=1) for h in heads]
        st[u]["rs"] = [jnp.sum(ls[h], axis=-1, keepdims=True) for h in heads]
        st[u]["lb"] = lb

    def suffix(u):
        hl = st[u].pop("hl")
        st[u]["suf"] = [jnp.dot(hl[h], later2, preferred_element_type=f32) for h in heads]

    def weight(u):
        qi, j = u
        lb, suf, rs = st[u].pop("lb"), st[u].pop("suf"), st[u].pop("rs")
        a = []
        for h in heads:
            t = lb[h] + suf[h]
            if j == qi:
                ah = jnp.where(causal, jnp.exp2(t), 0.0)
                run_c[qi, h] = rs[h]
            else:
                ah = jnp.exp2(t + run_c[qi, h])
                run_c[qi, h] = run_c[qi, h] + rs[h]
            a.append(ah.astype(bf16))
        st[u]["a"] = a

    def value(u):
        qi, j = u
        a = st[u].pop("a")
        v = v_ref[j * TK:(j + 1) * TK, :]
        for h in heads:
            pv = jnp.dot(a[h], v, preferred_element_type=f32)
            acc[qi, h] = pv if j == qi else acc[qi, h] + pv
        if j == 0:
            o_ref[qi * TQ:(qi + 1) * TQ, :] = jnp.where(
                lane < HEAD_DIM, acc.pop((qi, 0)), acc.pop((qi, 1))).astype(o_ref.dtype)

    stages = (score, logs, suffix, weight, value)
    for t in range(len(units) + max(STICK_SKEW.values())):
        for stage in stages:
            i = t - STICK_SKEW[stage.__name__]
            if 0 <= i < len(units):
                stage(units[i])


def _attn_stick(proj):
    bsz = proj.shape[0]
    sec_blocks = SECTION // LANES
    return pl.pallas_call(
        _attn_stick_kernel,
        out_shape=jax.ShapeDtypeStruct((bsz, SEQ, SECTION), jnp.bfloat16),
        grid=(bsz, sec_blocks),
        in_specs=[
            pl.BlockSpec((None, SEQ, LANES), lambda b, h: (b, 0, 3 * sec_blocks + h)),
            pl.BlockSpec((None, SEQ, LANES), lambda b, h: (b, 0, 4 * sec_blocks + h)),
            pl.BlockSpec((None, SEQ, LANES), lambda b, h: (b, 0, 5 * sec_blocks + h)),
        ],
        out_specs=pl.BlockSpec((None, SEQ, LANES), lambda b, h: (b, 0, h)),
        compiler_params=pltpu.CompilerParams(
            dimension_semantics=("parallel", "parallel"), vmem_limit_bytes=VMEM_LIMIT),
        name="attn_stick",
    )(proj, proj, proj)


def _post_kernel(x_ref, ma_ref, mb_ref, p_ref, ge_ref, be_ref, wo_ref, g1_ref, b1_ref,
                 wg_ref, wu_ref, wd_ref, g2_ref, b2_ref, wpg_ref, bpg_ref, wpp_ref, g3_ref, b3_ref, o_ref):
    f32, bf16 = jnp.float32, jnp.bfloat16
    h = _ln(x_ref[...], ge_ref[...], be_ref[...])
    y = (jnp.dot(ma_ref[...], wo_ref[0:SECTION, :], preferred_element_type=f32)
         + jnp.dot(mb_ref[...], wo_ref[SECTION:2 * SECTION, :], preferred_element_type=f32))
    h = _ln(DEEPNORM_ALPHA * h + y, g1_ref[...], b1_ref[...])

    hb = h.astype(bf16)
    f = jnp.zeros(h.shape, f32)
    for c in range(D_FF // FF_CHUNK):
        cols = slice(c * FF_CHUNK, (c + 1) * FF_CHUNK)
        gate = jnp.dot(hb, wg_ref[:, cols], preferred_element_type=f32)
        up = jnp.dot(hb, wu_ref[:, cols], preferred_element_type=f32)
        act = (gate * jax.nn.sigmoid(gate) * up).astype(bf16)
        f = f + jnp.dot(act, wd_ref[cols, :], preferred_element_type=f32)
    h = _ln(DEEPNORM_ALPHA * h + f, g2_ref[...], b2_ref[...])

    gate = jax.nn.sigmoid(jnp.dot(h.astype(bf16), wpg_ref[...], preferred_element_type=f32) + bpg_ref[...])
    e = jnp.dot(p_ref[...].astype(bf16), wpp_ref[...], preferred_element_type=f32)
    o_ref[...] = _ln(DEEPNORM_ALPHA * h + gate * e, g3_ref[...], b3_ref[...])


def _post(x2d, mix_a, mix_b, p2d, vecs, mats):
    n = x2d.shape[0]
    ge, be, g1, b1, g2, b2, bpg, g3, b3 = vecs
    wo, wg, wu, wd, wpg, wpp = mats
    row = lambda w: pl.BlockSpec((TM_POST, w), lambda i: (i, 0))
    vec = _resident((1, D_MODEL))
    return pl.pallas_call(
        _post_kernel,
        out_shape=jax.ShapeDtypeStruct((n, D_MODEL), jnp.float32),
        grid=(n // TM_POST,),
        in_specs=[row(D_MODEL), row(SECTION), row(SECTION), row(PLE_DIM), vec, vec,
                  _resident(wo.shape), vec, vec,
                  _resident(wg.shape), _resident(wu.shape), _resident(wd.shape), vec, vec,
                  _resident(wpg.shape), vec, _resident(wpp.shape), vec, vec],
        out_specs=row(D_MODEL),
        compiler_params=pltpu.CompilerParams(
            dimension_semantics=("parallel",), vmem_limit_bytes=VMEM_LIMIT),
        name="post",
    )(x2d, mix_a, mix_b, p2d, ge, be, wo, g1, b1, wg, wu, wd, g2, b2, wpg, bpg, wpp, g3, b3)


def kernel(x, p, ln_emb_g, ln_emb_b, w_in, lam_q1, lam_k1, lam_q2, lam_k2, subln_g, w_out, ln1_g, ln1_b,
           w_ffn_gate, w_ffn_up, w_ffn_down, ln2_g, ln2_b, w_ple_gate, b_ple_gate, w_ple_proj, ln3_g, ln3_b):
    bsz, seq, d = x.shape
    assert (seq, d) == (SEQ, D_MODEL) and w_in.shape == (DEPTH, D_MODEL, IN_WIDTH)
    bf16 = jnp.bfloat16
    row = lambda v: v.reshape(1, -1)
    x2d = x.reshape(bsz * seq, d)

    proj = _in_proj(x2d, row(ln_emb_g), row(ln_emb_b), w_in[0].astype(bf16), _rope_tables())
    proj = proj.reshape(bsz, seq, IN_WIDTH)
    lam_params = jnp.stack([lam_q1[0], lam_k1[0], lam_q2[0], lam_k2[0]])
    mix_a = _attn_diff(proj, lam_params, row(subln_g[0]))
    mix_b = _attn_stick(proj)

    vecs = (row(ln_emb_g), row(ln_emb_b), ln1_g, ln1_b, ln2_g, ln2_b, b_ple_gate, ln3_g, ln3_b)
    mats = tuple(w[0].astype(bf16) for w in (w_out, w_ffn_gate, w_ffn_up, w_ffn_down, w_ple_gate, w_ple_proj))
    out = _post(x2d, mix_a.reshape(bsz * seq, SECTION), mix_b.reshape(bsz * seq, SECTION),
                p[0].reshape(bsz * seq, PLE_DIM), vecs, mats)
    return out.reshape(bsz, seq, d)
```

```python
import functools
import math

import jax
import jax.numpy as jnp
from jax import lax
from jax.experimental import pallas as pl
from jax.experimental.pallas import tpu as pltpu

D_MODEL = 1024
SEQ = 2048
DEPTH = 1
CHUNK = 64
HEAD_DIM = 64
A_HEADS = 4
B_HEADS = 8
SECTION = 512
IN_WIDTH = 6 * SECTION
D_FF = 2816
PLE_DIM = 256
ROPE_THETA = 10000.0
LN_EPS = 1e-5
DEEPNORM_ALPHA = (2.0 * DEPTH) ** 0.25
LAMBDA_INIT = 0.8 - 0.6 * math.exp(-0.3 * 0)
Q_SCALE = math.log2(math.e) / math.sqrt(HEAD_DIM)

LANES = 128
VMEM_LIMIT = 56 * 1024 * 1024

TM_PROJ = 512
TM_POST = 512
FF_CHUNK = 256
TQ = 256
TK = 256
NQ = SEQ // TQ

_NT = (((1,), (1,)), ((), ()))


def _ln(x, g, b):
    mu = jnp.mean(x, axis=-1, keepdims=True)
    xc = x - mu
    var = jnp.mean(xc * xc, axis=-1, keepdims=True)
    return xc * lax.rsqrt(var + LN_EPS) * g + b


def _resident(shape):
    return pl.BlockSpec(shape, lambda *_: (0,) * len(shape), pipeline_mode=pl.Buffered(1))


def _split_lanes(q):
    lane = lax.broadcasted_iota(jnp.int32, (1, LANES), 1)
    zero = jnp.zeros_like(q)
    return jnp.where(lane < HEAD_DIM, q, zero), jnp.where(lane >= HEAD_DIM, q, zero)


def _merge(*lists):
    keyed = [((i + 0.5) / len(l), li, i, item) for li, l in enumerate(lists) for i, item in enumerate(l)]
    keyed.sort(key=lambda t: t[:3])
    return [t[3] for t in keyed]


def _in_proj_kernel(x_ref, g_ref, b_ref, w_ref, rope_ref, o_ref):
    hb = _ln(x_ref[...], g_ref[...], b_ref[...]).astype(jnp.bfloat16)
    for sec in range(6):
        acc = jnp.dot(hb, w_ref[:, sec * SECTION:(sec + 1) * SECTION],
                      preferred_element_type=jnp.float32)
        if sec < 2:
            cos = rope_ref[3 * sec + 0]
            sin_lo = rope_ref[3 * sec + 1]
            sin_hi = rope_ref[3 * sec + 2]
            for c in range(SECTION // LANES):
                t = acc[:, c * LANES:(c + 1) * LANES]
                r = (t * cos + pltpu.roll(t, LANES - HEAD_DIM // 2, 1) * sin_lo
                     + pltpu.roll(t, HEAD_DIM // 2, 1) * sin_hi)
                o_ref[:, sec * SECTION + c * LANES: sec * SECTION + (c + 1) * LANES] = r.astype(o_ref.dtype)
        else:
            if sec == 3:
                acc = acc * Q_SCALE
            o_ref[:, sec * SECTION:(sec + 1) * SECTION] = acc.astype(o_ref.dtype)


def _rope_tables():
    half = HEAD_DIM // 2
    inv_freq = 1.0 / (ROPE_THETA ** (jnp.arange(0, HEAD_DIM, 2, dtype=jnp.float32) / HEAD_DIM))
    ang = jnp.arange(SEQ, dtype=jnp.float32)[:, None] * inv_freq[None, :]
    ang = jnp.concatenate([ang, ang], axis=-1)
    cos, sin = jnp.cos(ang), jnp.sin(ang)
    first = (jnp.arange(HEAD_DIM) < half)[None, :]
    sin_lo = jnp.where(first, -sin, 0.0)
    sin_hi = jnp.where(first, 0.0, sin)
    k_tabs = jnp.stack([jnp.tile(t, (1, LANES // HEAD_DIM)) for t in (cos, sin_lo, sin_hi)])
    return jnp.concatenate([k_tabs * Q_SCALE, k_tabs], axis=0)


def _in_proj(x2d, g, b, w_bf16, rope):
    n = x2d.shape[0]
    seq_tiles = SEQ // TM_PROJ
    return pl.pallas_call(
        _in_proj_kernel,
        out_shape=jax.ShapeDtypeStruct((n, IN_WIDTH), jnp.bfloat16),
        grid=(n // TM_PROJ,),
        in_specs=[
            pl.BlockSpec((TM_PROJ, D_MODEL), lambda i: (i, 0)),
            _resident((1, D_MODEL)),
            _resident((1, D_MODEL)),
            _resident((D_MODEL, IN_WIDTH)),
            pl.BlockSpec((6, TM_PROJ, LANES), lambda i: (0, i % seq_tiles, 0)),
        ],
        out_specs=pl.BlockSpec((TM_PROJ, IN_WIDTH), lambda i: (i, 0)),
        compiler_params=pltpu.CompilerParams(
            dimension_semantics=("parallel",), vmem_limit_bytes=VMEM_LIMIT),
        name="in_proj",
    )(x2d, g, b, w_bf16, rope)


def _attn_diff_kernel(lam_ref, q_ref, k_ref, v_ref, g_ref, o_ref):
    f32, bf16 = jnp.float32, jnp.bfloat16
    lp = lam_ref[...]
    lam = (jnp.exp(jnp.sum(lp[0:1] * lp[1:2], axis=-1, keepdims=True))
           - jnp.exp(jnp.sum(lp[2:3] * lp[3:4], axis=-1, keepdims=True)) + LAMBDA_INIT)
    visible = (lax.broadcasted_iota(jnp.int32, (TQ, TK), 0) // CHUNK
               >= lax.broadcasted_iota(jnp.int32, (TQ, TK), 1) // CHUNK)
    maps = range(2)
    qm, s, mx, p, psum, ratio, inv_l1, w = {}, {}, {}, {}, {}, {}, {}, {}

    def score(qi, j):
        if qi not in qm:
            qm[qi] = _split_lanes(q_ref[qi * TQ:(qi + 1) * TQ, :])
        k = k_ref[j * TK:(j + 1) * TK, :]
        for m in maps:
            sj = lax.dot_general(qm[qi][m], k, _NT, preferred_element_type=f32)
            if j == qi:
                sj = jnp.where(visible, sj, -jnp.inf)
            s[qi, j, m] = sj

    def rowmax(qi):
        for m in maps:
            blk = functools.reduce(jnp.maximum, [s[qi, j, m] for j in range(qi + 1)])
            mx[qi, m] = jnp.max(blk, axis=-1, keepdims=True)

    def prob(qi, j):
        for m in maps:
            pj = jnp.exp2(s.pop((qi, j, m)) - mx[qi, m])
            p[qi, j, m] = pj
            psum[qi, m] = pj if j == 0 else psum[qi, m] + pj

    def rowsum(qi):
        l1 = jnp.sum(psum.pop((qi, 0)), axis=-1, keepdims=True)
        l2 = jnp.sum(psum.pop((qi, 1)), axis=-1, keepdims=True)
        inv_l1[qi] = 1.0 / l1
        ratio[qi] = lam * l1 / l2

    def weight(qi, j):
        w[qi, j] = (p.pop((qi, j, 0)) - ratio[qi] * p.pop((qi, j, 1))).astype(bf16)

    def out(qi):
        wp = jnp.concatenate([w.pop((qi, j)) for j in range(qi + 1)], axis=1)
        o = jnp.dot(wp, v_ref[0:(qi + 1) * TK, :], preferred_element_type=f32) * inv_l1.pop(qi)
        y = o * lax.rsqrt(jnp.mean(o * o, axis=-1, keepdims=True) + LN_EPS) * g_ref[...]
        o_ref[qi * TQ:(qi + 1) * TQ, :] = (y * (1.0 - LAMBDA_INIT)).astype(o_ref.dtype)

    for t in range(NQ + 2):
        scoring = [(score, (t, j)) for j in range(t + 1)] if t < NQ else []
        softmax = ([(rowmax, (t - 1,))] + [(prob, (t - 1, j)) for j in range(t)]
                   + [(rowsum, (t - 1,))]) if 1 <= t <= NQ else []
        mixing = ([(weight, (t - 2, j)) for j in range(t - 1)] + [(out, (t - 2,))]) if t >= 2 else []
        for fn, args in _merge(*[l for l in (scoring, softmax, mixing) if l]):
            fn(*args)


def _attn_diff(proj, lam_params, subln_g):
    bsz = proj.shape[0]
    sec_blocks = SECTION // LANES
    return pl.pallas_call(
        _attn_diff_kernel,
        out_shape=jax.ShapeDtypeStruct((bsz, SEQ, A_HEADS * LANES), jnp.bfloat16),
        grid=(bsz, A_HEADS),
        in_specs=[
            _resident((4, HEAD_DIM)),
            pl.BlockSpec((None, SEQ, LANES), lambda b, h: (b, 0, h)),
            pl.BlockSpec((None, SEQ, LANES), lambda b, h: (b, 0, sec_blocks + h)),
            pl.BlockSpec((None, SEQ, LANES), lambda b, h: (b, 0, 2 * sec_blocks + h)),
            _resident((1, LANES)),
        ],
        out_specs=pl.BlockSpec((None, SEQ, LANES), lambda b, h: (b, 0, h)),
        compiler_params=pltpu.CompilerParams(
            dimension_semantics=("parallel", "parallel"), vmem_limit_bytes=VMEM_LIMIT),
        name="attn_diff",
    )(lam_params, proj, proj, proj, subln_g)


STICK_ORDER = (("score", 0), ("suffix", 1), ("logs", 0), ("weight", 1), ("value", 1))
STICK_DEAD = 150.0


def _stick_logs(z):
    mn = jnp.minimum(z, 0.0)
    lg = jnp.log2(1.0 + jnp.exp2(mn + mn - z))
    lb = mn - lg
    return lb, lb - z


def _attn_stick_kernel(q_ref, k_ref, v_ref, o_ref, acc_ref, c_ref):
    f32, bf16 = jnp.float32, jnp.bfloat16
    lane = lax.broadcasted_iota(jnp.int32, (1, LANES), 1)
    causal = (lax.broadcasted_iota(jnp.int32, (TQ, TK), 1)
              < lax.broadcasted_iota(jnp.int32, (TQ, TK), 0))
    later = (lax.broadcasted_iota(jnp.int32, (TK, TK), 0)
             > lax.broadcasted_iota(jnp.int32, (TK, TK), 1)).astype(bf16)
    heads = range(2)

    units = [(qi, j) for qi in range(NQ) for j in range(qi, max(qi - 2, -1), -1)]
    st = {u: {} for u in units}
    qs, run_c, acc = {}, {}, {}

    def score(u):
        qi, j = u
        if qi not in qs:
            qs[qi] = _split_lanes(q_ref[qi * TQ:(qi + 1) * TQ, :])
        k = k_ref[j * TK:(j + 1) * TK, :]
        st[u]["z"] = [lax.dot_general(qs[qi][h], k, _NT, preferred_element_type=f32) for h in heads]

    def logs(u):
        qi, j = u
        z = st[u].pop("z")
        terms = [_stick_logs(z[h]) for h in heads]
        ls = [terms[h][1] for h in heads]
        if j == qi:
            ls = [jnp.where(causal, ls[h], 0.0) for h in heads]
        st[u]["lsb"] = [ls[h].astype(bf16) for h in heads]
        st[u]["rs"] = [jnp.sum(ls[h], axis=-1, keepdims=True) for h in heads]
        st[u]["lb"] = [terms[h][0] for h in heads]

    def suffix(u):
        lsb = st[u].pop("lsb")
        st[u]["suf"] = [jnp.dot(lsb[h], later, preferred_element_type=f32) for h in heads]

    def weight(u):
        qi, j = u
        lb, suf, rs = st[u].pop("lb"), st[u].pop("suf"), st[u].pop("rs")
        a = []
        for h in heads:
            t = lb[h] + suf[h]
            if j == qi:
                ah = jnp.where(causal, jnp.exp2(t), 0.0)
                run_c[qi, h] = rs[h]
            else:
                ah = jnp.exp2(t + run_c[qi, h])
                run_c[qi, h] = run_c[qi, h] + rs[h]
            a.append(ah.astype(bf16))
        st[u]["a"] = a

    def value(u):
        qi, j = u
        a = st[u].pop("a")
        v = v_ref[j * TK:(j + 1) * TK, :]
        for h in heads:
            pv = jnp.dot(a[h], v, preferred_element_type=f32)
            acc[qi, h] = pv if j == qi else acc[qi, h] + pv
        if j == max(qi - 1, 0):
            for h in heads:
                acc_ref[qi, h] = acc.pop((qi, h))
                c_ref[qi, h] = run_c.pop((qi, h))

    stages = dict(score=score, logs=logs, suffix=suffix, weight=weight, value=value)
    for t in range(len(units) + max(d for _, d in STICK_ORDER)):
        for name, delay in STICK_ORDER:
            if 0 <= t - delay < len(units):
                stages[name](units[t - delay])

    def tail_step(qi, i, carry):
        j = qi - 2 - i
        rows = pl.ds(pl.multiple_of(j * TK, TK), TK)
        k, v = k_ref[rows, :], v_ref[rows, :]
        q_pair = _split_lanes(q_ref[qi * TQ:(qi + 1) * TQ, :])
        out = []
        for h in heads:
            c, acc_h = carry[h]
            lb, ls = _stick_logs(lax.dot_general(q_pair[h], k, _NT, preferred_element_type=f32))
            suf = jnp.dot(ls.astype(bf16), later, preferred_element_type=f32)
            a = jnp.exp2(lb + suf + c)
            acc_h = acc_h + jnp.dot(a.astype(bf16), v, preferred_element_type=f32)
            out.append((c + jnp.sum(ls, axis=-1, keepdims=True), acc_h))
        return tuple(out)

    for qi in range(2, NQ):
        alive = jnp.max(jnp.maximum(c_ref[qi, 0], c_ref[qi, 1])) > -STICK_DEAD

        @pl.when(alive)
        def _(qi=qi):
            init = tuple((c_ref[qi, h], acc_ref[qi, h]) for h in heads)
            done = lax.fori_loop(0, qi - 1, functools.partial(tail_step, qi), init)
            for h in heads:
                acc_ref[qi, h] = done[h][1]

    for qi in range(NQ):
        o_ref[qi * TQ:(qi + 1) * TQ, :] = jnp.where(
            lane < HEAD_DIM, acc_ref[qi, 0], acc_ref[qi, 1]).astype(o_ref.dtype)


def _attn_stick(proj):
    bsz = proj.shape[0]
    sec_blocks = SECTION // LANES
    return pl.pallas_call(
        _attn_stick_kernel,
        out_shape=jax.ShapeDtypeStruct((bsz, SEQ, SECTION), jnp.bfloat16),
        grid=(bsz, sec_blocks),
        in_specs=[
            pl.BlockSpec((None, SEQ, LANES), lambda b, h: (b, 0, 3 * sec_blocks + h)),
            pl.BlockSpec((None, SEQ, LANES), lambda b, h: (b, 0, 4 * sec_blocks + h)),
            pl.BlockSpec((None, SEQ, LANES), lambda b, h: (b, 0, 5 * sec_blocks + h)),
        ],
        out_specs=pl.BlockSpec((None, SEQ, LANES), lambda b, h: (b, 0, h)),
        scratch_shapes=[pltpu.VMEM((NQ, 2, TQ, LANES), jnp.float32), pltpu.VMEM((NQ, 2, TQ, 1), jnp.float32)],
        compiler_params=pltpu.CompilerParams(
            dimension_semantics=("parallel", "parallel"), vmem_limit_bytes=VMEM_LIMIT),
        name="attn_stick",
    )(proj, proj, proj)


def _post_kernel(x_ref, ma_ref, mb_ref, p_ref, ge_ref, be_ref, wo_ref, g1_ref, b1_ref,
                 wg_ref, wu_ref, wd_ref, g2_ref, b2_ref, wpg_ref, bpg_ref, wpp_ref, g3_ref, b3_ref, o_ref):
    f32, bf16 = jnp.float32, jnp.bfloat16
    h = _ln(x_ref[...], ge_ref[...], be_ref[...])
    y = (jnp.dot(ma_ref[...], wo_ref[0:SECTION, :], preferred_element_type=f32)
         + jnp.dot(mb_ref[...], wo_ref[SECTION:2 * SECTION, :], preferred_element_type=f32))
    h = _ln(DEEPNORM_ALPHA * h + y, g1_ref[...], b1_ref[...])

    hb = h.astype(bf16)
    f = jnp.zeros(h.shape, f32)
    for c in range(D_FF // FF_CHUNK):
        cols = slice(c * FF_CHUNK, (c + 1) * FF_CHUNK)
        gate = jnp.dot(hb, wg_ref[:, cols], preferred_element_type=f32)
        up = jnp.dot(hb, wu_ref[:, cols], preferred_element_type=f32)
        act = (gate * jax.nn.sigmoid(gate) * up).astype(bf16)
        f = f + jnp.dot(act, wd_ref[cols, :], preferred_element_type=f32)
    h = _ln(DEEPNORM_ALPHA * h + f, g2_ref[...], b2_ref[...])

    gate = jax.nn.sigmoid(jnp.dot(h.astype(bf16), wpg_ref[...], preferred_element_type=f32) + bpg_ref[...])
    e = jnp.dot(p_ref[...].astype(bf16), wpp_ref[...], preferred_element_type=f32)
    o_ref[...] = _ln(DEEPNORM_ALPHA * h + gate * e, g3_ref[...], b3_ref[...])


def _post(x2d, mix_a, mix_b, p2d, vecs, mats):
    n = x2d.shape[0]
    ge, be, g1, b1, g2, b2, bpg, g3, b3 = vecs
    wo, wg, wu, wd, wpg, wpp = mats
    row = lambda w: pl.BlockSpec((TM_POST, w), lambda i: (i, 0))
    vec = _resident((1, D_MODEL))
    return pl.pallas_call(
        _post_kernel,
        out_shape=jax.ShapeDtypeStruct((n, D_MODEL), jnp.float32),
        grid=(n // TM_POST,),
        in_specs=[row(D_MODEL), row(SECTION), row(SECTION), row(PLE_DIM), vec, vec,
                  _resident(wo.shape), vec, vec,
                  _resident(wg.shape), _resident(wu.shape), _resident(wd.shape), vec, vec,
                  _resident(wpg.shape), vec, _resident(wpp.shape), vec, vec],
        out_specs=row(D_MODEL),
        compiler_params=pltpu.CompilerParams(
            dimension_semantics=("parallel",), vmem_limit_bytes=VMEM_LIMIT),
        name="post",
    )(x2d, mix_a, mix_b, p2d, ge, be, wo, g1, b1, wg, wu, wd, g2, b2, wpg, bpg, wpp, g3, b3)


def kernel(x, p, ln_emb_g, ln_emb_b, w_in, lam_q1, lam_k1, lam_q2, lam_k2, subln_g, w_out, ln1_g, ln1_b,
           w_ffn_gate, w_ffn_up, w_ffn_down, ln2_g, ln2_b, w_ple_gate, b_ple_gate, w_ple_proj, ln3_g, ln3_b):
    bsz, seq, d = x.shape
    assert (seq, d) == (SEQ, D_MODEL) and w_in.shape == (DEPTH, D_MODEL, IN_WIDTH)
    bf16 = jnp.bfloat16
    row = lambda v: v.reshape(1, -1)
    x2d = x.reshape(bsz * seq, d)

    proj = _in_proj(x2d, row(ln_emb_g), row(ln_emb_b), w_in[0].astype(bf16), _rope_tables())
    proj = proj.reshape(bsz, seq, IN_WIDTH)
    lam_params = jnp.stack([lam_q1[0], lam_k1[0], lam_q2[0], lam_k2[0]])
    mix_a = _attn_diff(proj, lam_params, row(subln_g[0]))
    mix_b = _attn_stick(proj)

    vecs = (row(ln_emb_g), row(ln_emb_b), ln1_g, ln1_b, ln2_g, ln2_b, b_ple_gate, ln3_g, ln3_b)
    mats = tuple(w[0].astype(bf16) for w in (w_out, w_ffn_gate, w_ffn_up, w_ffn_down, w_ple_gate, w_ple_proj))
    out = _post(x2d, mix_a.reshape(bsz * seq, SECTION), mix_b.reshape(bsz * seq, SECTION),
                p[0].reshape(bsz * seq, PLE_DIM), vecs, mats)
    return out.reshape(bsz, seq, d)
```

```python
import functools
import math

import jax
import jax.numpy as jnp
from jax import lax
from jax.experimental import pallas as pl
from jax.experimental.pallas import tpu as pltpu

D_MODEL = 1024
SEQ = 2048
DEPTH = 1
CHUNK = 64
HEAD_DIM = 64
A_HEADS = 4
B_HEADS = 8
SECTION = 512
IN_WIDTH = 6 * SECTION
D_FF = 2816
PLE_DIM = 256
ROPE_THETA = 10000.0
LN_EPS = 1e-5
DEEPNORM_ALPHA = (2.0 * DEPTH) ** 0.25
LAMBDA_INIT = 0.8 - 0.6 * math.exp(-0.3 * 0)
Q_SCALE = math.log2(math.e) / math.sqrt(HEAD_DIM)

LANES = 128
VMEM_LIMIT = 56 * 1024 * 1024

TM_PROJ = 512
TM_POST = 512
FF_CHUNK = 256
TQ = 256
TK = 256
NQ = SEQ // TQ

_NT = (((1,), (1,)), ((), ()))


def _ln(x, g, b):
    mu = jnp.mean(x, axis=-1, keepdims=True)
    xc = x - mu
    var = jnp.mean(xc * xc, axis=-1, keepdims=True)
    return xc * lax.rsqrt(var + LN_EPS) * g + b


def _resident(shape):
    return pl.BlockSpec(shape, lambda *_: (0,) * len(shape), pipeline_mode=pl.Buffered(1))


def _split_lanes(q):
    lane = lax.broadcasted_iota(jnp.int32, (1, LANES), 1)
    zero = jnp.zeros_like(q)
    return jnp.where(lane < HEAD_DIM, q, zero), jnp.where(lane >= HEAD_DIM, q, zero)


def _merge(*lists):
    keyed = [((i + 0.5) / len(l), li, i, item) for li, l in enumerate(lists) for i, item in enumerate(l)]
    keyed.sort(key=lambda t: t[:3])
    return [t[3] for t in keyed]


def _in_proj_kernel(x_ref, g_ref, b_ref, w_ref, rope_ref, o_ref):
    hb = _ln(x_ref[...], g_ref[...], b_ref[...]).astype(jnp.bfloat16)
    for sec in range(6):
        acc = jnp.dot(hb, w_ref[:, sec * SECTION:(sec + 1) * SECTION],
                      preferred_element_type=jnp.float32)
        if sec < 2:
            cos = rope_ref[3 * sec + 0]
            sin_lo = rope_ref[3 * sec + 1]
            sin_hi = rope_ref[3 * sec + 2]
            for c in range(SECTION // LANES):
                t = acc[:, c * LANES:(c + 1) * LANES]
                r = (t * cos + pltpu.roll(t, LANES - HEAD_DIM // 2, 1) * sin_lo
                     + pltpu.roll(t, HEAD_DIM // 2, 1) * sin_hi)
                o_ref[:, sec * SECTION + c * LANES: sec * SECTION + (c + 1) * LANES] = r.astype(o_ref.dtype)
        else:
            if sec == 3:
                acc = acc * Q_SCALE
            o_ref[:, sec * SECTION:(sec + 1) * SECTION] = acc.astype(o_ref.dtype)


def _rope_tables():
    half = HEAD_DIM // 2
    inv_freq = 1.0 / (ROPE_THETA ** (jnp.arange(0, HEAD_DIM, 2, dtype=jnp.float32) / HEAD_DIM))
    ang = jnp.arange(SEQ, dtype=jnp.float32)[:, None] * inv_freq[None, :]
    ang = jnp.concatenate([ang, ang], axis=-1)
    cos, sin = jnp.cos(ang), jnp.sin(ang)
    first = (jnp.arange(HEAD_DIM) < half)[None, :]
    sin_lo = jnp.where(first, -sin, 0.0)
    sin_hi = jnp.where(first, 0.0, sin)
    k_tabs = jnp.stack([jnp.tile(t, (1, LANES // HEAD_DIM)) for t in (cos, sin_lo, sin_hi)])
    return jnp.concatenate([k_tabs * Q_SCALE, k_tabs], axis=0)


def _in_proj(x2d, g, b, w_bf16, rope):
    n = x2d.shape[0]
    seq_tiles = SEQ // TM_PROJ
    return pl.pallas_call(
        _in_proj_kernel,
        out_shape=jax.ShapeDtypeStruct((n, IN_WIDTH), jnp.bfloat16),
        grid=(n // TM_PROJ,),
        in_specs=[
            pl.BlockSpec((TM_PROJ, D_MODEL), lambda i: (i, 0)),
            _resident((1, D_MODEL)),
            _resident((1, D_MODEL)),
            _resident((D_MODEL, IN_WIDTH)),
            pl.BlockSpec((6, TM_PROJ, LANES), lambda i: (0, i % seq_tiles, 0)),
        ],
        out_specs=pl.BlockSpec((TM_PROJ, IN_WIDTH), lambda i: (i, 0)),
        compiler_params=pltpu.CompilerParams(
            dimension_semantics=("parallel",), vmem_limit_bytes=VMEM_LIMIT),
        name="in_proj",
    )(x2d, g, b, w_bf16, rope)


DIFF_ORDER = (0, 2, 4, 6, 7, 5, 3, 1)


def _attn_diff_kernel(lam_ref, q_ref, k_ref, v_ref, g_ref, o_ref):
    f32, bf16 = jnp.float32, jnp.bfloat16
    lp = lam_ref[...]
    lam = (jnp.exp(jnp.sum(lp[0:1] * lp[1:2], axis=-1, keepdims=True))
           - jnp.exp(jnp.sum(lp[2:3] * lp[3:4], axis=-1, keepdims=True)) + LAMBDA_INIT)
    visible = (lax.broadcasted_iota(jnp.int32, (TQ, TK), 0) // CHUNK
               >= lax.broadcasted_iota(jnp.int32, (TQ, TK), 1) // CHUNK)
    maps = range(2)
    qm, s, mx, p, psum, ratio, inv_l1, w = {}, {}, {}, {}, {}, {}, {}, {}

    def score(qi, j):
        if qi not in qm:
            qm[qi] = _split_lanes(q_ref[qi * TQ:(qi + 1) * TQ, :])
        k = k_ref[j * TK:(j + 1) * TK, :]
        for m in maps:
            sj = lax.dot_general(qm[qi][m], k, _NT, preferred_element_type=f32)
            if j == qi:
                sj = jnp.where(visible, sj, -jnp.inf)
            s[qi, j, m] = sj

    def rowmax(qi):
        for m in maps:
            blk = functools.reduce(jnp.maximum, [s[qi, j, m] for j in range(qi + 1)])
            mx[qi, m] = jnp.max(blk, axis=-1, keepdims=True)

    def prob(qi, j):
        for m in maps:
            pj = jnp.exp2(s.pop((qi, j, m)) - mx[qi, m])
            p[qi, j, m] = pj
            psum[qi, m] = pj if j == 0 else psum[qi, m] + pj

    def rowsum(qi):
        l1 = jnp.sum(psum.pop((qi, 0)), axis=-1, keepdims=True)
        l2 = jnp.sum(psum.pop((qi, 1)), axis=-1, keepdims=True)
        inv_l1[qi] = 1.0 / l1
        ratio[qi] = lam * l1 / l2

    def weight(qi, j):
        w[qi, j] = (p.pop((qi, j, 0)) - ratio[qi] * p.pop((qi, j, 1))).astype(bf16)

    def out(qi):
        wp = jnp.concatenate([w.pop((qi, j)) for j in range(qi + 1)], axis=1)
        o = jnp.dot(wp, v_ref[0:(qi + 1) * TK, :], preferred_element_type=f32) * inv_l1.pop(qi)
        y = o * lax.rsqrt(jnp.mean(o * o, axis=-1, keepdims=True) + LN_EPS) * g_ref[...]
        o_ref[qi * TQ:(qi + 1) * TQ, :] = (y * (1.0 - LAMBDA_INIT)).astype(o_ref.dtype)

    order = DIFF_ORDER
    for t in range(NQ + 2):
        scoring = [(score, (order[t], j)) for j in range(order[t] + 1)] if t < NQ else []
        softmax = ([(rowmax, (order[t - 1],))] + [(prob, (order[t - 1], j)) for j in range(order[t - 1] + 1)]
                   + [(rowsum, (order[t - 1],))]) if 1 <= t <= NQ else []
        mixing = ([(weight, (order[t - 2], j)) for j in range(order[t - 2] + 1)]
                  + [(out, (order[t - 2],))]) if t >= 2 else []
        for fn, args in _merge(*[l for l in (scoring, softmax, mixing) if l]):
            fn(*args)


def _attn_diff(proj, lam_params, subln_g):
    bsz = proj.shape[0]
    sec_blocks = SECTION // LANES
    return pl.pallas_call(
        _attn_diff_kernel,
        out_shape=jax.ShapeDtypeStruct((bsz, SEQ, A_HEADS * LANES), jnp.bfloat16),
        grid=(bsz, A_HEADS),
        in_specs=[
            _resident((4, HEAD_DIM)),
            pl.BlockSpec((None, SEQ, LANES), lambda b, h: (b, 0, h)),
            pl.BlockSpec((None, SEQ, LANES), lambda b, h: (b, 0, sec_blocks + h)),
            pl.BlockSpec((None, SEQ, LANES), lambda b, h: (b, 0, 2 * sec_blocks + h)),
            _resident((1, LANES)),
        ],
        out_specs=pl.BlockSpec((None, SEQ, LANES), lambda b, h: (b, 0, h)),
        compiler_params=pltpu.CompilerParams(
            dimension_semantics=("parallel", "parallel"), vmem_limit_bytes=VMEM_LIMIT),
        name="attn_diff",
    )(lam_params, proj, proj, proj, subln_g)


STICK_ORDER = (("score", 0), ("suffix", 1), ("logs", 0), ("weight", 1), ("value", 1))
STICK_DEAD = 150.0


def _stick_logs(z):
    mn = jnp.minimum(z, 0.0)
    lg = jnp.log2(1.0 + jnp.exp2(mn + mn - z))
    lb = mn - lg
    return lb, lb - z


def _attn_stick_kernel(q_ref, k_ref, v_ref, o_ref, acc_ref, c_ref):
    f32, bf16 = jnp.float32, jnp.bfloat16
    lane = lax.broadcasted_iota(jnp.int32, (1, LANES), 1)
    causal = (lax.broadcasted_iota(jnp.int32, (TQ, TK), 1)
              < lax.broadcasted_iota(jnp.int32, (TQ, TK), 0))
    later = (lax.broadcasted_iota(jnp.int32, (TK, TK), 0)
             > lax.broadcasted_iota(jnp.int32, (TK, TK), 1)).astype(bf16)
    heads = range(2)

    units = [(qi, j) for qi in range(NQ) for j in range(qi, max(qi - 2, -1), -1)]
    st = {u: {} for u in units}
    qs, run_c, acc = {}, {}, {}

    def score(u):
        qi, j = u
        if qi not in qs:
            qs[qi] = _split_lanes(q_ref[qi * TQ:(qi + 1) * TQ, :])
        k = k_ref[j * TK:(j + 1) * TK, :]
        st[u]["z"] = [lax.dot_general(qs[qi][h], k, _NT, preferred_element_type=f32) for h in heads]

    def logs(u):
        qi, j = u
        z = st[u].pop("z")
        terms = [_stick_logs(z[h]) for h in heads]
        ls = [terms[h][1] for h in heads]
        if j == qi:
            ls = [jnp.where(causal, ls[h], 0.0) for h in heads]
        st[u]["lsb"] = [ls[h].astype(bf16) for h in heads]
        st[u]["rs"] = [jnp.sum(ls[h], axis=-1, keepdims=True) for h in heads]
        st[u]["lb"] = [terms[h][0] for h in heads]

    def suffix(u):
        lsb = st[u].pop("lsb")
        st[u]["suf"] = [jnp.dot(lsb[h], later, preferred_element_type=f32) for h in heads]

    def weight(u):
        qi, j = u
        lb, suf, rs = st[u].pop("lb"), st[u].pop("suf"), st[u].pop("rs")
        a = []
        for h in heads:
            t = lb[h] + suf[h]
            if j == qi:
                ah = jnp.where(causal, jnp.exp2(t), 0.0)
                run_c[qi, h] = rs[h]
            else:
                ah = jnp.exp2(t + run_c[qi, h])
                run_c[qi, h] = run_c[qi, h] + rs[h]
            a.append(ah.astype(bf16))
        st[u]["a"] = a

    def value(u):
        qi, j = u
        a = st[u].pop("a")
        v = v_ref[j * TK:(j + 1) * TK, :]
        for h in heads:
            pv = jnp.dot(a[h], v, preferred_element_type=f32)
            acc[qi, h] = pv if j == qi else acc[qi, h] + pv
        if j == max(qi - 1, 0):
            for h in heads:
                acc_ref[qi, h] = acc.pop((qi, h))
                c_ref[qi, h] = run_c[qi, h]

    stages = dict(score=score, logs=logs, suffix=suffix, weight=weight, value=value)
    for t in range(len(units) + max(d for _, d in STICK_ORDER)):
        for name, delay in STICK_ORDER:
            if 0 <= t - delay < len(units):
                stages[name](units[t - delay])

    def tail_step(qi, i, carry):
        j = qi - 2 - i
        rows = pl.ds(pl.multiple_of(j * TK, TK), TK)
        k, v = k_ref[rows, :], v_ref[rows, :]
        q_pair = _split_lanes(q_ref[qi * TQ:(qi + 1) * TQ, :])
        out = []
        for h in heads:
            c, acc_h = carry[h]
            lb, ls = _stick_logs(lax.dot_general(q_pair[h], k, _NT, preferred_element_type=f32))
            suf = jnp.dot(ls.astype(bf16), later, preferred_element_type=f32)
            a = jnp.exp2(lb + suf + c)
            acc_h = acc_h + jnp.dot(a.astype(bf16), v, preferred_element_type=f32)
            out.append((c + jnp.sum(ls, axis=-1, keepdims=True), acc_h))
        return tuple(out)

    def remainder(qi):
        init = tuple((c_ref[qi, h], acc_ref[qi, h]) for h in heads)
        done = lax.fori_loop(0, qi - 1, functools.partial(tail_step, qi), init)
        for h in heads:
            acc_ref[qi, h] = done[h][1]

    least_dead = functools.reduce(jnp.maximum, [run_c[qi, h] for qi in range(2, NQ) for h in heads])

    @pl.when(jnp.max(least_dead) > -STICK_DEAD)
    def _():
        for qi in range(2, NQ):
            pl.when(jnp.max(jnp.maximum(c_ref[qi, 0], c_ref[qi, 1])) > -STICK_DEAD)(
                functools.partial(remainder, qi))

    for qi in range(NQ):
        o_ref[qi * TQ:(qi + 1) * TQ, :] = jnp.where(
            lane < HEAD_DIM, acc_ref[qi, 0], acc_ref[qi, 1]).astype(o_ref.dtype)


def _attn_stick(proj):
    bsz = proj.shape[0]
    sec_blocks = SECTION // LANES
    return pl.pallas_call(
        _attn_stick_kernel,
        out_shape=jax.ShapeDtypeStruct((bsz, SEQ, SECTION), jnp.bfloat16),
        grid=(bsz, sec_blocks),
        in_specs=[
            pl.BlockSpec((None, SEQ, LANES), lambda b, h: (b, 0, 3 * sec_blocks + h)),
            pl.BlockSpec((None, SEQ, LANES), lambda b, h: (b, 0, 4 * sec_blocks + h)),
            pl.BlockSpec((None, SEQ, LANES), lambda b, h: (b, 0, 5 * sec_blocks + h)),
        ],
        out_specs=pl.BlockSpec((None, SEQ, LANES), lambda b, h: (b, 0, h)),
        scratch_shapes=[pltpu.VMEM((NQ, 2, TQ, LANES), jnp.float32), pltpu.VMEM((NQ, 2, TQ, 1), jnp.float32)],
        compiler_params=pltpu.CompilerParams(
            dimension_semantics=("parallel", "parallel"), vmem_limit_bytes=VMEM_LIMIT),
        name="attn_stick",
    )(proj, proj, proj)


def _post_kernel(x_ref, ma_ref, mb_ref, p_ref, ge_ref, be_ref, wo_ref, g1_ref, b1_ref,
                 wg_ref, wu_ref, wd_ref, g2_ref, b2_ref, wpg_ref, bpg_ref, wpp_ref, g3_ref, b3_ref, o_ref):
    f32, bf16 = jnp.float32, jnp.bfloat16
    h = _ln(x_ref[...], ge_ref[...], be_ref[...])
    y = (jnp.dot(ma_ref[...], wo_ref[0:SECTION, :], preferred_element_type=f32)
         + jnp.dot(mb_ref[...], wo_ref[SECTION:2 * SECTION, :], preferred_element_type=f32))
    h = _ln(DEEPNORM_ALPHA * h + y, g1_ref[...], b1_ref[...])

    hb = h.astype(bf16)
    f = jnp.zeros(h.shape, f32)
    for c in range(D_FF // FF_CHUNK):
        cols = slice(c * FF_CHUNK, (c + 1) * FF_CHUNK)
        gate = jnp.dot(hb, wg_ref[:, cols], preferred_element_type=f32)
        up = jnp.dot(hb, wu_ref[:, cols], preferred_element_type=f32)
        act = (gate * jax.nn.sigmoid(gate) * up).astype(bf16)
        f = f + jnp.dot(act, wd_ref[cols, :], preferred_element_type=f32)
    h = _ln(DEEPNORM_ALPHA * h + f, g2_ref[...], b2_ref[...])

    gate = jax.nn.sigmoid(jnp.dot(h.astype(bf16), wpg_ref[...], preferred_element_type=f32) + bpg_ref[...])
    e = jnp.dot(p_ref[...].astype(bf16), wpp_ref[...], preferred_element_type=f32)
    o_ref[...] = _ln(DEEPNORM_ALPHA * h + gate * e, g3_ref[...], b3_ref[...])


def _post(x2d, mix_a, mix_b, p2d, vecs, mats):
    n = x2d.shape[0]
    ge, be, g1, b1, g2, b2, bpg, g3, b3 = vecs
    wo, wg, wu, wd, wpg, wpp = mats
    row = lambda w: pl.BlockSpec((TM_POST, w), lambda i: (i, 0))
    vec = _resident((1, D_MODEL))
    return pl.pallas_call(
        _post_kernel,
        out_shape=jax.ShapeDtypeStruct((n, D_MODEL), jnp.float32),
        grid=(n // TM_POST,),
        in_specs=[row(D_MODEL), row(SECTION), row(SECTION), row(PLE_DIM), vec, vec,
                  _resident(wo.shape), vec, vec,
                  _resident(wg.shape), _resident(wu.shape), _resident(wd.shape), vec, vec,
                  _resident(wpg.shape), vec, _resident(wpp.shape), vec, vec],
        out_specs=row(D_MODEL),
        compiler_params=pltpu.CompilerParams(
            dimension_semantics=("parallel",), vmem_limit_bytes=VMEM_LIMIT),
        name="post",
    )(x2d, mix_a, mix_b, p2d, ge, be, wo, g1, b1, wg, wu, wd, g2, b2, wpg, bpg, wpp, g3, b3)


def kernel(x, p, ln_emb_g, ln_emb_b, w_in, lam_q1, lam_k1, lam_q2, lam_k2, subln_g, w_out, ln1_g, ln1_b,
           w_ffn_gate, w_ffn_up, w_ffn_down, ln2_g, ln2_b, w_ple_gate, b_ple_gate, w_ple_proj, ln3_g, ln3_b):
    bsz, seq, d = x.shape
    assert (seq, d) == (SEQ, D_MODEL) and w_in.shape == (DEPTH, D_MODEL, IN_WIDTH)
    bf16 = jnp.bfloat16
    row = lambda v: v.reshape(1, -1)
    x2d = x.reshape(bsz * seq, d)

    proj = _in_proj(x2d, row(ln_emb_g), row(ln_emb_b), w_in[0].astype(bf16), _rope_tables())
    proj = proj.reshape(bsz, seq, IN_WIDTH)
    lam_params = jnp.stack([lam_q1[0], lam_k1[0], lam_q2[0], lam_k2[0]])
    mix_a = _attn_diff(proj, lam_params, row(subln_g[0]))
    mix_b = _attn_stick(proj)

    vecs = (row(ln_emb_g), row(ln_emb_b), ln1_g, ln1_b, ln2_g, ln2_b, b_ple_gate, ln3_g, ln3_b)
    mats = tuple(w[0].astype(bf16) for w in (w_out, w_ffn_gate, w_ffn_up, w_ffn_down, w_ple_gate, w_ple_proj))
    out = _post(x2d, mix_a.reshape(bsz * seq, SECTION), mix_b.reshape(bsz * seq, SECTION),
                p[0].reshape(bsz * seq, PLE_DIM), vecs, mats)
    return out.reshape(bsz, seq, d)
```

```python
import functools
import math

import jax
import jax.numpy as jnp
from jax import lax
from jax.experimental import pallas as pl
from jax.experimental.pallas import tpu as pltpu

D_MODEL = 1024
SEQ = 2048
DEPTH = 1
CHUNK = 64
HEAD_DIM = 64
A_HEADS = 4
B_HEADS = 8
SECTION = 512
IN_WIDTH = 6 * SECTION
D_FF = 2816
PLE_DIM = 256
ROPE_THETA = 10000.0
LN_EPS = 1e-5
DEEPNORM_ALPHA = (2.0 * DEPTH) ** 0.25
LAMBDA_INIT = 0.8 - 0.6 * math.exp(-0.3 * 0)
Q_SCALE = math.log2(math.e) / math.sqrt(HEAD_DIM)

LANES = 128
VMEM_LIMIT = 56 * 1024 * 1024
VMEM_LIMIT_POST = 60 * 1024 * 1024

TM_PROJ = 1024
TM_POST = 1024
SUB_ROWS = 512
FF_CHUNK = 256
TQ = 256
TK = 256
NQ = SEQ // TQ

_NT = (((1,), (1,)), ((), ()))


def _ln(x, g, b):
    mu = jnp.mean(x, axis=-1, keepdims=True)
    xc = x - mu
    var = jnp.mean(xc * xc, axis=-1, keepdims=True)
    return xc * lax.rsqrt(var + LN_EPS) * g + b


def _resident(shape):
    return pl.BlockSpec(shape, lambda *_: (0,) * len(shape), pipeline_mode=pl.Buffered(1))


def _split_lanes(q):
    lane = lax.broadcasted_iota(jnp.int32, (1, LANES), 1)
    zero = jnp.zeros_like(q)
    return jnp.where(lane < HEAD_DIM, q, zero), jnp.where(lane >= HEAD_DIM, q, zero)


def _merge(*lists):
    keyed = [((i + 0.5) / len(l), li, i, item) for li, l in enumerate(lists) for i, item in enumerate(l)]
    keyed.sort(key=lambda t: t[:3])
    return [t[3] for t in keyed]


def _in_proj_kernel(x_ref, g_ref, b_ref, w_ref, rope_ref, o_ref):
    n_sub = TM_PROJ // SUB_ROWS
    hb = {}

    def norm(r):
        rows = slice(r * SUB_ROWS, (r + 1) * SUB_ROWS)
        hb[r] = _ln(x_ref[rows, :], g_ref[...], b_ref[...]).astype(jnp.bfloat16)

    def project(r, sec):
        rows = slice(r * SUB_ROWS, (r + 1) * SUB_ROWS)
        acc = jnp.dot(hb[r], w_ref[:, sec * SECTION:(sec + 1) * SECTION], preferred_element_type=jnp.float32)
        if sec < 2:
            cos = rope_ref[3 * sec + 0, rows, :]
            sin_lo = rope_ref[3 * sec + 1, rows, :]
            sin_hi = rope_ref[3 * sec + 2, rows, :]
            for c in range(SECTION // LANES):
                cols = slice(sec * SECTION + c * LANES, sec * SECTION + (c + 1) * LANES)
                t = acc[:, c * LANES:(c + 1) * LANES]
                rot = (t * cos + pltpu.roll(t, LANES - HEAD_DIM // 2, 1) * sin_lo
                       + pltpu.roll(t, HEAD_DIM // 2, 1) * sin_hi)
                o_ref[rows, cols] = rot.astype(o_ref.dtype)
        else:
            if sec == 3:
                acc = acc * Q_SCALE
            o_ref[rows, sec * SECTION:(sec + 1) * SECTION] = acc.astype(o_ref.dtype)

    norm(0)
    for r in range(n_sub):
        lanes = [[(project, (r, sec)) for sec in range(6)]]
        if r + 1 < n_sub:
            lanes.append([(norm, (r + 1,))])
        for fn, args in _merge(*lanes):
            fn(*args)


def _rope_tables():
    half = HEAD_DIM // 2
    inv_freq = 1.0 / (ROPE_THETA ** (jnp.arange(0, HEAD_DIM, 2, dtype=jnp.float32) / HEAD_DIM))
    ang = jnp.arange(SEQ, dtype=jnp.float32)[:, None] * inv_freq[None, :]
    ang = jnp.concatenate([ang, ang], axis=-1)
    cos, sin = jnp.cos(ang), jnp.sin(ang)
    first = (jnp.arange(HEAD_DIM) < half)[None, :]
    sin_lo = jnp.where(first, -sin, 0.0)
    sin_hi = jnp.where(first, 0.0, sin)
    k_tabs = jnp.stack([jnp.tile(t, (1, LANES // HEAD_DIM)) for t in (cos, sin_lo, sin_hi)])
    return jnp.concatenate([k_tabs * Q_SCALE, k_tabs], axis=0)


def _in_proj(x2d, g, b, w_bf16, rope):
    n = x2d.shape[0]
    seq_tiles = SEQ // TM_PROJ
    return pl.pallas_call(
        _in_proj_kernel,
        out_shape=jax.ShapeDtypeStruct((n, IN_WIDTH), jnp.bfloat16),
        grid=(n // TM_PROJ,),
        in_specs=[
            pl.BlockSpec((TM_PROJ, D_MODEL), lambda i: (i, 0)),
            _resident((1, D_MODEL)),
            _resident((1, D_MODEL)),
            _resident((D_MODEL, IN_WIDTH)),
            pl.BlockSpec((6, TM_PROJ, LANES), lambda i: (0, i % seq_tiles, 0)),
        ],
        out_specs=pl.BlockSpec((TM_PROJ, IN_WIDTH), lambda i: (i, 0)),
        compiler_params=pltpu.CompilerParams(
            dimension_semantics=("parallel",), vmem_limit_bytes=VMEM_LIMIT),
        name="in_proj",
    )(x2d, g, b, w_bf16, rope)


DIFF_ORDER = (0, 2, 4, 6, 7, 5, 3, 1)


def _attn_diff_kernel(lam_ref, q_ref, k_ref, v_ref, g_ref, o_ref):
    f32, bf16 = jnp.float32, jnp.bfloat16
    lp = lam_ref[...]
    lam = (jnp.exp(jnp.sum(lp[0:1] * lp[1:2], axis=-1, keepdims=True))
           - jnp.exp(jnp.sum(lp[2:3] * lp[3:4], axis=-1, keepdims=True)) + LAMBDA_INIT)
    visible = (lax.broadcasted_iota(jnp.int32, (TQ, TK), 0) // CHUNK
               >= lax.broadcasted_iota(jnp.int32, (TQ, TK), 1) // CHUNK)
    maps = range(2)
    qm, s, mx, p, psum, ratio, inv_l1, w = {}, {}, {}, {}, {}, {}, {}, {}

    def score(qi, j):
        if qi not in qm:
            qm[qi] = _split_lanes(q_ref[qi * TQ:(qi + 1) * TQ, :])
        k = k_ref[j * TK:(j + 1) * TK, :]
        for m in maps:
            sj = lax.dot_general(qm[qi][m], k, _NT, preferred_element_type=f32)
            if j == qi:
                sj = jnp.where(visible, sj, -jnp.inf)
            s[qi, j, m] = sj

    def rowmax(qi):
        for m in maps:
            blk = functools.reduce(jnp.maximum, [s[qi, j, m] for j in range(qi + 1)])
            mx[qi, m] = jnp.max(blk, axis=-1, keepdims=True)

    def prob(qi, j):
        for m in maps:
            pj = jnp.exp2(s.pop((qi, j, m)) - mx[qi, m])
            p[qi, j, m] = pj
            psum[qi, m] = pj if j == 0 else psum[qi, m] + pj

    def rowsum(qi):
        l1 = jnp.sum(psum.pop((qi, 0)), axis=-1, keepdims=True)
        l2 = jnp.sum(psum.pop((qi, 1)), axis=-1, keepdims=True)
        inv_l1[qi] = 1.0 / l1
        ratio[qi] = lam * l1 / l2

    def weight(qi, j):
        w[qi, j] = (p.pop((qi, j, 0)) - ratio[qi] * p.pop((qi, j, 1))).astype(bf16)

    def out(qi):
        wp = jnp.concatenate([w.pop((qi, j)) for j in range(qi + 1)], axis=1)
        o = jnp.dot(wp, v_ref[0:(qi + 1) * TK, :], preferred_element_type=f32) * inv_l1.pop(qi)
        y = o * lax.rsqrt(jnp.mean(o * o, axis=-1, keepdims=True) + LN_EPS) * g_ref[...]
        o_ref[qi * TQ:(qi + 1) * TQ, :] = (y * (1.0 - LAMBDA_INIT)).astype(o_ref.dtype)

    order = DIFF_ORDER
    for t in range(NQ + 2):
        scoring = [(score, (order[t], j)) for j in range(order[t] + 1)] if t < NQ else []
        softmax = ([(rowmax, (order[t - 1],))] + [(prob, (order[t - 1], j)) for j in range(order[t - 1] + 1)]
                   + [(rowsum, (order[t - 1],))]) if 1 <= t <= NQ else []
        mixing = ([(weight, (order[t - 2], j)) for j in range(order[t - 2] + 1)]
                  + [(out, (order[t - 2],))]) if t >= 2 else []
        for fn, args in _merge(*[l for l in (scoring, softmax, mixing) if l]):
            fn(*args)


def _attn_diff(proj, lam_params, subln_g):
    bsz = proj.shape[0]
    sec_blocks = SECTION // LANES
    return pl.pallas_call(
        _attn_diff_kernel,
        out_shape=jax.ShapeDtypeStruct((bsz, SEQ, A_HEADS * LANES), jnp.bfloat16),
        grid=(bsz, A_HEADS),
        in_specs=[
            _resident((4, HEAD_DIM)),
            pl.BlockSpec((None, SEQ, LANES), lambda b, h: (b, 0, h)),
            pl.BlockSpec((None, SEQ, LANES), lambda b, h: (b, 0, sec_blocks + h)),
            pl.BlockSpec((None, SEQ, LANES), lambda b, h: (b, 0, 2 * sec_blocks + h)),
            _resident((1, LANES)),
        ],
        out_specs=pl.BlockSpec((None, SEQ, LANES), lambda b, h: (b, 0, h)),
        compiler_params=pltpu.CompilerParams(
            dimension_semantics=("parallel", "parallel"), vmem_limit_bytes=VMEM_LIMIT),
        name="attn_diff",
    )(lam_params, proj, proj, proj, subln_g)


STICK_ORDER = (("score", 0), ("suffix", 1), ("logs", 0), ("weight", 1), ("value", 1))
STICK_DEAD = 150.0


def _stick_logs(z):
    mn = jnp.minimum(z, 0.0)
    lg = jnp.log2(1.0 + jnp.exp2(mn + mn - z))
    lb = mn - lg
    return lb, lb - z


def _attn_stick_kernel(q_ref, k_ref, v_ref, o_ref, acc_ref, c_ref):
    f32, bf16 = jnp.float32, jnp.bfloat16
    lane = lax.broadcasted_iota(jnp.int32, (1, LANES), 1)
    causal = (lax.broadcasted_iota(jnp.int32, (TQ, TK), 1)
              < lax.broadcasted_iota(jnp.int32, (TQ, TK), 0))
    later = (lax.broadcasted_iota(jnp.int32, (TK, TK), 0)
             > lax.broadcasted_iota(jnp.int32, (TK, TK), 1)).astype(bf16)
    heads = range(2)

    units = [(qi, j) for qi in range(NQ) for j in range(qi, max(qi - 2, -1), -1)]
    st = {u: {} for u in units}
    qs, run_c, acc = {}, {}, {}

    def score(u):
        qi, j = u
        if qi not in qs:
            qs[qi] = _split_lanes(q_ref[qi * TQ:(qi + 1) * TQ, :])
        k = k_ref[j * TK:(j + 1) * TK, :]
        st[u]["z"] = [lax.dot_general(qs[qi][h], k, _NT, preferred_element_type=f32) for h in heads]

    def logs(u):
        qi, j = u
        z = st[u].pop("z")
        terms = [_stick_logs(z[h]) for h in heads]
        ls = [terms[h][1] for h in heads]
        if j == qi:
            ls = [jnp.where(causal, ls[h], 0.0) for h in heads]
        st[u]["lsb"] = [ls[h].astype(bf16) for h in heads]
        st[u]["rs"] = [jnp.sum(ls[h], axis=-1, keepdims=True) for h in heads]
        st[u]["lb"] = [terms[h][0] for h in heads]

    def suffix(u):
        lsb = st[u].pop("lsb")
        st[u]["suf"] = [jnp.dot(lsb[h], later, preferred_element_type=f32) for h in heads]

    def weight(u):
        qi, j = u
        lb, suf, rs = st[u].pop("lb"), st[u].pop("suf"), st[u].pop("rs")
        a = []
        for h in heads:
            t = lb[h] + suf[h]
            if j == qi:
                ah = jnp.where(causal, jnp.exp2(t), 0.0)
                run_c[qi, h] = rs[h]
            else:
                ah = jnp.exp2(t + run_c[qi, h])
                run_c[qi, h] = run_c[qi, h] + rs[h]
            a.append(ah.astype(bf16))
        st[u]["a"] = a

    def value(u):
        qi, j = u
        a = st[u].pop("a")
        v = v_ref[j * TK:(j + 1) * TK, :]
        for h in heads:
            pv = jnp.dot(a[h], v, preferred_element_type=f32)
            acc[qi, h] = pv if j == qi else acc[qi, h] + pv
        if j == max(qi - 1, 0):
            for h in heads:
                acc_ref[qi, h] = acc.pop((qi, h))
                c_ref[qi, h] = run_c[qi, h]

    stages = dict(score=score, logs=logs, suffix=suffix, weight=weight, value=value)
    for t in range(len(units) + max(d for _, d in STICK_ORDER)):
        for name, delay in STICK_ORDER:
            if 0 <= t - delay < len(units):
                stages[name](units[t - delay])

    def tail_step(qi, i, carry):
        j = qi - 2 - i
        rows = pl.ds(pl.multiple_of(j * TK, TK), TK)
        k, v = k_ref[rows, :], v_ref[rows, :]
        q_pair = _split_lanes(q_ref[qi * TQ:(qi + 1) * TQ, :])
        out = []
        for h in heads:
            c, acc_h = carry[h]
            lb, ls = _stick_logs(lax.dot_general(q_pair[h], k, _NT, preferred_element_type=f32))
            suf = jnp.dot(ls.astype(bf16), later, preferred_element_type=f32)
            a = jnp.exp2(lb + suf + c)
            acc_h = acc_h + jnp.dot(a.astype(bf16), v, preferred_element_type=f32)
            out.append((c + jnp.sum(ls, axis=-1, keepdims=True), acc_h))
        return tuple(out)

    def remainder(qi):
        init = tuple((c_ref[qi, h], acc_ref[qi, h]) for h in heads)
        done = lax.fori_loop(0, qi - 1, functools.partial(tail_step, qi), init)
        for h in heads:
            acc_ref[qi, h] = done[h][1]

    least_dead = functools.reduce(jnp.maximum, [run_c[qi, h] for qi in range(2, NQ) for h in heads])

    @pl.when(jnp.max(least_dead) > -STICK_DEAD)
    def _():
        for qi in range(2, NQ):
            pl.when(jnp.max(jnp.maximum(c_ref[qi, 0], c_ref[qi, 1])) > -STICK_DEAD)(
                functools.partial(remainder, qi))

    for qi in range(NQ):
        o_ref[qi * TQ:(qi + 1) * TQ, :] = jnp.where(
            lane < HEAD_DIM, acc_ref[qi, 0], acc_ref[qi, 1]).astype(o_ref.dtype)


def _attn_stick(proj):
    bsz = proj.shape[0]
    sec_blocks = SECTION // LANES
    return pl.pallas_call(
        _attn_stick_kernel,
        out_shape=jax.ShapeDtypeStruct((bsz, SEQ, SECTION), jnp.bfloat16),
        grid=(bsz, sec_blocks),
        in_specs=[
            pl.BlockSpec((None, SEQ, LANES), lambda b, h: (b, 0, 3 * sec_blocks + h)),
            pl.BlockSpec((None, SEQ, LANES), lambda b, h: (b, 0, 4 * sec_blocks + h)),
            pl.BlockSpec((None, SEQ, LANES), lambda b, h: (b, 0, 5 * sec_blocks + h)),
        ],
        out_specs=pl.BlockSpec((None, SEQ, LANES), lambda b, h: (b, 0, h)),
        scratch_shapes=[pltpu.VMEM((NQ, 2, TQ, LANES), jnp.float32), pltpu.VMEM((NQ, 2, TQ, 1), jnp.float32)],
        compiler_params=pltpu.CompilerParams(
            dimension_semantics=("parallel", "parallel"), vmem_limit_bytes=VMEM_LIMIT),
        name="attn_stick",
    )(proj, proj, proj)


def _post_kernel(x_ref, ma_ref, mb_ref, p_ref, ge_ref, be_ref, wo_ref, g1_ref, b1_ref,
                 wg_ref, wu_ref, wd_ref, g2_ref, b2_ref, wpg_ref, bpg_ref, wpp_ref, g3_ref, b3_ref, o_ref):
    f32, bf16 = jnp.float32, jnp.bfloat16
    n_sub = TM_POST // SUB_ROWS
    n_chunks = D_FF // FF_CHUNK
    st = [dict() for _ in range(n_sub)]

    def mix(r):
        rows = slice(r * SUB_ROWS, (r + 1) * SUB_ROWS)
        h = _ln(x_ref[rows, :], ge_ref[...], be_ref[...])
        y = (jnp.dot(ma_ref[rows, :], wo_ref[0:SECTION, :], preferred_element_type=f32)
             + jnp.dot(mb_ref[rows, :], wo_ref[SECTION:2 * SECTION, :], preferred_element_type=f32))
        h = _ln(DEEPNORM_ALPHA * h + y, g1_ref[...], b1_ref[...])
        st[r]["h"] = h
        st[r]["hb"] = h.astype(bf16)

    def ffn(r, c):
        cols = slice(c * FF_CHUNK, (c + 1) * FF_CHUNK)
        hb = st[r]["hb"]
        gate = jnp.dot(hb, wg_ref[:, cols], preferred_element_type=f32)
        up = jnp.dot(hb, wu_ref[:, cols], preferred_element_type=f32)
        act = (gate * jax.nn.sigmoid(gate) * up).astype(bf16)
        d = jnp.dot(act, wd_ref[cols, :], preferred_element_type=f32)
        st[r]["f"] = d if c == 0 else st[r]["f"] + d

    def ple(r):
        rows = slice(r * SUB_ROWS, (r + 1) * SUB_ROWS)
        s = st[r]
        s.pop("hb")
        h = _ln(DEEPNORM_ALPHA * s.pop("h") + s.pop("f"), g2_ref[...], b2_ref[...])
        gate = jax.nn.sigmoid(jnp.dot(h.astype(bf16), wpg_ref[...], preferred_element_type=f32) + bpg_ref[...])
        e = jnp.dot(p_ref[rows, :].astype(bf16), wpp_ref[...], preferred_element_type=f32)
        o_ref[rows, :] = _ln(DEEPNORM_ALPHA * h + gate * e, g3_ref[...], b3_ref[...])

    for r in range(n_sub + 1):
        ffn_prev = [(ffn, (r - 1, c)) for c in range(n_chunks)] if r >= 1 else []
        norms = ([(ple, (r - 2,))] if r >= 2 else []) + ([(mix, (r,))] if r < n_sub else [])
        for fn, args in _merge(*[l for l in (ffn_prev, norms) if l]):
            fn(*args)
    ple(n_sub - 1)


def _post(x2d, mix_a, mix_b, p2d, vecs, mats):
    n = x2d.shape[0]
    ge, be, g1, b1, g2, b2, bpg, g3, b3 = vecs
    wo, wg, wu, wd, wpg, wpp = mats
    row = lambda w: pl.BlockSpec((TM_POST, w), lambda i: (i, 0))
    vec = _resident((1, D_MODEL))
    return pl.pallas_call(
        _post_kernel,
        out_shape=jax.ShapeDtypeStruct((n, D_MODEL), jnp.float32),
        grid=(n // TM_POST,),
        in_specs=[row(D_MODEL), row(SECTION), row(SECTION), row(PLE_DIM), vec, vec,
                  _resident(wo.shape), vec, vec,
                  _resident(wg.shape), _resident(wu.shape), _resident(wd.shape), vec, vec,
                  _resident(wpg.shape), vec, _resident(wpp.shape), vec, vec],
        out_specs=row(D_MODEL),
        compiler_params=pltpu.CompilerParams(
            dimension_semantics=("parallel",), vmem_limit_bytes=VMEM_LIMIT_POST),
        name="post",
    )(x2d, mix_a, mix_b, p2d, ge, be, wo, g1, b1, wg, wu, wd, g2, b2, wpg, bpg, wpp, g3, b3)


def kernel(x, p, ln_emb_g, ln_emb_b, w_in, lam_q1, lam_k1, lam_q2, lam_k2, subln_g, w_out, ln1_g, ln1_b,
           w_ffn_gate, w_ffn_up, w_ffn_down, ln2_g, ln2_b, w_ple_gate, b_ple_gate, w_ple_proj, ln3_g, ln3_b):
    bsz, seq, d = x.shape
    assert (seq, d) == (SEQ, D_MODEL) and w_in.shape == (DEPTH, D_MODEL, IN_WIDTH)
    bf16 = jnp.bfloat16
    row = lambda v: v.reshape(1, -1)
    x2d = x.reshape(bsz * seq, d)

    proj = _in_proj(x2d, row(ln_emb_g), row(ln_emb_b), w_in[0].astype(bf16), _rope_tables())
    proj = proj.reshape(bsz, seq, IN_WIDTH)
    lam_params = jnp.stack([lam_q1[0], lam_k1[0], lam_q2[0], lam_k2[0]])
    mix_a = _attn_diff(proj, lam_params, row(subln_g[0]))
    mix_b = _attn_stick(proj)

    vecs = (row(ln_emb_g), row(ln_emb_b), ln1_g, ln1_b, ln2_g, ln2_b, b_ple_gate, ln3_g, ln3_b)
    mats = tuple(w[0].astype(bf16) for w in (w_out, w_ffn_gate, w_ffn_up, w_ffn_down, w_ple_gate, w_ple_proj))
    out = _post(x2d, mix_a.reshape(bsz * seq, SECTION), mix_b.reshape(bsz * seq, SECTION),
                p[0].reshape(bsz * seq, PLE_DIM), vecs, mats)
    return out.reshape(bsz, seq, d)
```

```python
import functools
import math

import jax
import jax.numpy as jnp
from jax import lax
from jax.experimental import pallas as pl
from jax.experimental.pallas import tpu as pltpu

D_MODEL = 1024
SEQ = 2048
DEPTH = 1
CHUNK = 64
HEAD_DIM = 64
A_HEADS = 4
B_HEADS = 8
SECTION = 512
IN_WIDTH = 6 * SECTION
D_FF = 2816
PLE_DIM = 256
ROPE_THETA = 10000.0
LN_EPS = 1e-5
DEEPNORM_ALPHA = (2.0 * DEPTH) ** 0.25
LAMBDA_INIT = 0.8 - 0.6 * math.exp(-0.3 * 0)
Q_SCALE = math.log2(math.e) / math.sqrt(HEAD_DIM)

LANES = 128
VMEM_LIMIT = 56 * 1024 * 1024
VMEM_LIMIT_POST = 60 * 1024 * 1024

TM_PROJ = 1024
TM_POST = 1024
SUB_ROWS = 512
FF_CHUNK = 256
TQ = 256
TK = 256
NQ = SEQ // TQ

_NT = (((1,), (1,)), ((), ()))


def _ln(x, g, b):
    mu = jnp.mean(x, axis=-1, keepdims=True)
    xc = x - mu
    var = jnp.mean(xc * xc, axis=-1, keepdims=True)
    return xc * lax.rsqrt(var + LN_EPS) * g + b


def _resident(shape):
    return pl.BlockSpec(shape, lambda *_: (0,) * len(shape), pipeline_mode=pl.Buffered(1))


def _split_lanes(q):
    lane = lax.broadcasted_iota(jnp.int32, (1, LANES), 1)
    zero = jnp.zeros_like(q)
    return jnp.where(lane < HEAD_DIM, q, zero), jnp.where(lane >= HEAD_DIM, q, zero)


def _merge(*lists):
    keyed = [((i + 0.5) / len(l), li, i, item) for li, l in enumerate(lists) for i, item in enumerate(l)]
    keyed.sort(key=lambda t: t[:3])
    return [t[3] for t in keyed]


def _in_proj_kernel(x_ref, g_ref, b_ref, w_ref, rope_ref, o_ref):
    n_sub = TM_PROJ // SUB_ROWS
    hb = {}

    def norm(r):
        rows = slice(r * SUB_ROWS, (r + 1) * SUB_ROWS)
        hb[r] = _ln(x_ref[rows, :], g_ref[...], b_ref[...]).astype(jnp.bfloat16)

    def project(r, sec):
        rows = slice(r * SUB_ROWS, (r + 1) * SUB_ROWS)
        acc = jnp.dot(hb[r], w_ref[:, sec * SECTION:(sec + 1) * SECTION], preferred_element_type=jnp.float32)
        if sec < 2:
            cos = rope_ref[3 * sec + 0, rows, :]
            sin_lo = rope_ref[3 * sec + 1, rows, :]
            sin_hi = rope_ref[3 * sec + 2, rows, :]
            for c in range(SECTION // LANES):
                cols = slice(sec * SECTION + c * LANES, sec * SECTION + (c + 1) * LANES)
                t = acc[:, c * LANES:(c + 1) * LANES]
                rot = (t * cos + pltpu.roll(t, LANES - HEAD_DIM // 2, 1) * sin_lo
                       + pltpu.roll(t, HEAD_DIM // 2, 1) * sin_hi)
                o_ref[rows, cols] = rot.astype(o_ref.dtype)
        else:
            if sec == 3:
                acc = acc * Q_SCALE
            o_ref[rows, sec * SECTION:(sec + 1) * SECTION] = acc.astype(o_ref.dtype)

    norm(0)
    for r in range(n_sub):
        lanes = [[(project, (r, sec)) for sec in range(6)]]
        if r + 1 < n_sub:
            lanes.append([(norm, (r + 1,))])
        for fn, args in _merge(*lanes):
            fn(*args)


def _rope_tables():
    half = HEAD_DIM // 2
    inv_freq = 1.0 / (ROPE_THETA ** (jnp.arange(0, HEAD_DIM, 2, dtype=jnp.float32) / HEAD_DIM))
    ang = jnp.arange(SEQ, dtype=jnp.float32)[:, None] * inv_freq[None, :]
    ang = jnp.concatenate([ang, ang], axis=-1)
    cos, sin = jnp.cos(ang), jnp.sin(ang)
    first = (jnp.arange(HEAD_DIM) < half)[None, :]
    sin_lo = jnp.where(first, -sin, 0.0)
    sin_hi = jnp.where(first, 0.0, sin)
    k_tabs = jnp.stack([jnp.tile(t, (1, LANES // HEAD_DIM)) for t in (cos, sin_lo, sin_hi)])
    return jnp.concatenate([k_tabs * Q_SCALE, k_tabs], axis=0)


def _in_proj(x2d, g, b, w_bf16, rope):
    n = x2d.shape[0]
    seq_tiles = SEQ // TM_PROJ
    return pl.pallas_call(
        _in_proj_kernel,
        out_shape=jax.ShapeDtypeStruct((n, IN_WIDTH), jnp.bfloat16),
        grid=(n // TM_PROJ,),
        in_specs=[
            pl.BlockSpec((TM_PROJ, D_MODEL), lambda i: (i, 0)),
            _resident((1, D_MODEL)),
            _resident((1, D_MODEL)),
            _resident((D_MODEL, IN_WIDTH)),
            pl.BlockSpec((6, TM_PROJ, LANES), lambda i: (0, i % seq_tiles, 0)),
        ],
        out_specs=pl.BlockSpec((TM_PROJ, IN_WIDTH), lambda i: (i, 0)),
        compiler_params=pltpu.CompilerParams(
            dimension_semantics=("parallel",), vmem_limit_bytes=VMEM_LIMIT),
        name="in_proj",
    )(x2d, g, b, w_bf16, rope)


DIFF_ORDER = (0, 2, 4, 6, 7, 5, 3, 1)


def _attn_diff_kernel(lam_ref, q_ref, k_ref, v_ref, g_ref, o_ref):
    f32, bf16 = jnp.float32, jnp.bfloat16
    lp = lam_ref[...]
    lam = (jnp.exp(jnp.sum(lp[0:1] * lp[1:2], axis=-1, keepdims=True))
           - jnp.exp(jnp.sum(lp[2:3] * lp[3:4], axis=-1, keepdims=True)) + LAMBDA_INIT)
    visible = (lax.broadcasted_iota(jnp.int32, (TQ, TK), 0) // CHUNK
               >= lax.broadcasted_iota(jnp.int32, (TQ, TK), 1) // CHUNK)
    maps = range(2)
    qm, s, mx, p, psum, ratio, inv_l1, w = {}, {}, {}, {}, {}, {}, {}, {}

    def score(qi, j):
        if qi not in qm:
            qm[qi] = jnp.concatenate(_split_lanes(q_ref[qi * TQ:(qi + 1) * TQ, :]), axis=0)
        k = k_ref[j * TK:(j + 1) * TK, :]
        both = lax.dot_general(qm[qi], k, _NT, preferred_element_type=f32)
        for m in maps:
            sj = both[m * TQ:(m + 1) * TQ, :]
            if j == qi:
                sj = jnp.where(visible, sj, -jnp.inf)
            s[qi, j, m] = sj

    def rowmax(qi):
        for m in maps:
            blk = functools.reduce(jnp.maximum, [s[qi, j, m] for j in range(qi + 1)])
            mx[qi, m] = jnp.max(blk, axis=-1, keepdims=True)

    def prob(qi, j):
        for m in maps:
            pj = jnp.exp2(s.pop((qi, j, m)) - mx[qi, m])
            p[qi, j, m] = pj
            psum[qi, m] = pj if j == 0 else psum[qi, m] + pj

    def rowsum(qi):
        l1 = jnp.sum(psum.pop((qi, 0)), axis=-1, keepdims=True)
        l2 = jnp.sum(psum.pop((qi, 1)), axis=-1, keepdims=True)
        inv_l1[qi] = 1.0 / l1
        ratio[qi] = lam * l1 / l2

    def weight(qi, j):
        w[qi, j] = (p.pop((qi, j, 0)) - ratio[qi] * p.pop((qi, j, 1))).astype(bf16)

    def out(qi):
        wp = jnp.concatenate([w.pop((qi, j)) for j in range(qi + 1)], axis=1)
        o = jnp.dot(wp, v_ref[0:(qi + 1) * TK, :], preferred_element_type=f32) * inv_l1.pop(qi)
        y = o * lax.rsqrt(jnp.mean(o * o, axis=-1, keepdims=True) + LN_EPS) * g_ref[...]
        o_ref[qi * TQ:(qi + 1) * TQ, :] = (y * (1.0 - LAMBDA_INIT)).astype(o_ref.dtype)

    order = DIFF_ORDER
    for t in range(NQ + 2):
        scoring = [(score, (order[t], j)) for j in range(order[t] + 1)] if t < NQ else []
        softmax = ([(rowmax, (order[t - 1],))] + [(prob, (order[t - 1], j)) for j in range(order[t - 1] + 1)]
                   + [(rowsum, (order[t - 1],))]) if 1 <= t <= NQ else []
        mixing = ([(weight, (order[t - 2], j)) for j in range(order[t - 2] + 1)]
                  + [(out, (order[t - 2],))]) if t >= 2 else []
        for fn, args in _merge(*[l for l in (scoring, softmax, mixing) if l]):
            fn(*args)


def _attn_diff(proj, lam_params, subln_g):
    bsz = proj.shape[0]
    sec_blocks = SECTION // LANES
    return pl.pallas_call(
        _attn_diff_kernel,
        out_shape=jax.ShapeDtypeStruct((bsz, SEQ, A_HEADS * LANES), jnp.bfloat16),
        grid=(bsz, A_HEADS),
        in_specs=[
            _resident((4, HEAD_DIM)),
            pl.BlockSpec((None, SEQ, LANES), lambda b, h: (b, 0, h)),
            pl.BlockSpec((None, SEQ, LANES), lambda b, h: (b, 0, sec_blocks + h)),
            pl.BlockSpec((None, SEQ, LANES), lambda b, h: (b, 0, 2 * sec_blocks + h)),
            _resident((1, LANES)),
        ],
        out_specs=pl.BlockSpec((None, SEQ, LANES), lambda b, h: (b, 0, h)),
        compiler_params=pltpu.CompilerParams(
            dimension_semantics=("parallel", "parallel"), vmem_limit_bytes=VMEM_LIMIT),
        name="attn_diff",
    )(lam_params, proj, proj, proj, subln_g)


STICK_ORDER = (("score", 0), ("suffix", 1), ("logs", 0), ("weight", 1), ("value", 1))
STICK_DEAD = 150.0


def _stick_logs(z):
    mn = jnp.minimum(z, 0.0)
    lg = jnp.log2(1.0 + jnp.exp2(mn + mn - z))
    lb = mn - lg
    return lb, lb - z


def _attn_stick_kernel(q_ref, k_ref, v_ref, o_ref, acc_ref, c_ref):
    f32, bf16 = jnp.float32, jnp.bfloat16
    lane = lax.broadcasted_iota(jnp.int32, (1, LANES), 1)
    causal = (lax.broadcasted_iota(jnp.int32, (TQ, TK), 1)
              < lax.broadcasted_iota(jnp.int32, (TQ, TK), 0))
    later = (lax.broadcasted_iota(jnp.int32, (TK, TK), 0)
             > lax.broadcasted_iota(jnp.int32, (TK, TK), 1)).astype(bf16)
    heads = range(2)

    units = [(qi, j) for qi in range(NQ) for j in range(qi, max(qi - 2, -1), -1)]
    st = {u: {} for u in units}
    qs, run_c, acc = {}, {}, {}

    def score(u):
        qi, j = u
        if qi not in qs:
            qs[qi] = jnp.concatenate(_split_lanes(q_ref[qi * TQ:(qi + 1) * TQ, :]), axis=0)
        k = k_ref[j * TK:(j + 1) * TK, :]
        both = lax.dot_general(qs[qi], k, _NT, preferred_element_type=f32)
        st[u]["z"] = [both[h * TQ:(h + 1) * TQ, :] for h in heads]

    def logs(u):
        qi, j = u
        z = st[u].pop("z")
        terms = [_stick_logs(z[h]) for h in heads]
        ls = [terms[h][1] for h in heads]
        if j == qi:
            ls = [jnp.where(causal, ls[h], 0.0) for h in heads]
        st[u]["lsb"] = [ls[h].astype(bf16) for h in heads]
        st[u]["rs"] = [jnp.sum(ls[h], axis=-1, keepdims=True) for h in heads]
        st[u]["lb"] = [terms[h][0] for h in heads]

    def suffix(u):
        lsb = st[u].pop("lsb")
        both = jnp.dot(jnp.concatenate(lsb, axis=0), later, preferred_element_type=f32)
        st[u]["suf"] = [both[h * TQ:(h + 1) * TQ, :] for h in heads]

    def weight(u):
        qi, j = u
        lb, suf, rs = st[u].pop("lb"), st[u].pop("suf"), st[u].pop("rs")
        a = []
        for h in heads:
            t = lb[h] + suf[h]
            if j == qi:
                ah = jnp.where(causal, jnp.exp2(t), 0.0)
                run_c[qi, h] = rs[h]
            else:
                ah = jnp.exp2(t + run_c[qi, h])
                run_c[qi, h] = run_c[qi, h] + rs[h]
            a.append(ah.astype(bf16))
        st[u]["a"] = a

    def value(u):
        qi, j = u
        a = st[u].pop("a")
        v = v_ref[j * TK:(j + 1) * TK, :]
        both = jnp.dot(jnp.concatenate(a, axis=0), v, preferred_element_type=f32)
        for h in heads:
            pv = both[h * TQ:(h + 1) * TQ, :]
            acc[qi, h] = pv if j == qi else acc[qi, h] + pv
        if j == max(qi - 1, 0):
            for h in heads:
                acc_ref[qi, h] = acc.pop((qi, h))
                c_ref[qi, h] = run_c[qi, h]

    stages = dict(score=score, logs=logs, suffix=suffix, weight=weight, value=value)
    for t in range(len(units) + max(d for _, d in STICK_ORDER)):
        for name, delay in STICK_ORDER:
            if 0 <= t - delay < len(units):
                stages[name](units[t - delay])

    def tail_step(qi, i, carry):
        j = qi - 2 - i
        rows = pl.ds(pl.multiple_of(j * TK, TK), TK)
        k, v = k_ref[rows, :], v_ref[rows, :]
        q_pair = _split_lanes(q_ref[qi * TQ:(qi + 1) * TQ, :])
        out = []
        for h in heads:
            c, acc_h = carry[h]
            lb, ls = _stick_logs(lax.dot_general(q_pair[h], k, _NT, preferred_element_type=f32))
            suf = jnp.dot(ls.astype(bf16), later, preferred_element_type=f32)
            a = jnp.exp2(lb + suf + c)
            acc_h = acc_h + jnp.dot(a.astype(bf16), v, preferred_element_type=f32)
            out.append((c + jnp.sum(ls, axis=-1, keepdims=True), acc_h))
        return tuple(out)

    def remainder(qi):
        init = tuple((c_ref[qi, h], acc_ref[qi, h]) for h in heads)
        done = lax.fori_loop(0, qi - 1, functools.partial(tail_step, qi), init)
        for h in heads:
            acc_ref[qi, h] = done[h][1]

    least_dead = functools.reduce(jnp.maximum, [run_c[qi, h] for qi in range(2, NQ) for h in heads])

    @pl.when(jnp.max(least_dead) > -STICK_DEAD)
    def _():
        for qi in range(2, NQ):
            pl.when(jnp.max(jnp.maximum(c_ref[qi, 0], c_ref[qi, 1])) > -STICK_DEAD)(
                functools.partial(remainder, qi))

    for qi in range(NQ):
        o_ref[qi * TQ:(qi + 1) * TQ, :] = jnp.where(
            lane < HEAD_DIM, acc_ref[qi, 0], acc_ref[qi, 1]).astype(o_ref.dtype)


def _attn_stick(proj):
    bsz = proj.shape[0]
    sec_blocks = SECTION // LANES
    return pl.pallas_call(
        _attn_stick_kernel,
        out_shape=jax.ShapeDtypeStruct((bsz, SEQ, SECTION), jnp.bfloat16),
        grid=(bsz, sec_blocks),
        in_specs=[
            pl.BlockSpec((None, SEQ, LANES), lambda b, h: (b, 0, 3 * sec_blocks + h)),
            pl.BlockSpec((None, SEQ, LANES), lambda b, h: (b, 0, 4 * sec_blocks + h)),
            pl.BlockSpec((None, SEQ, LANES), lambda b, h: (b, 0, 5 * sec_blocks + h)),
        ],
        out_specs=pl.BlockSpec((None, SEQ, LANES), lambda b, h: (b, 0, h)),
        scratch_shapes=[pltpu.VMEM((NQ, 2, TQ, LANES), jnp.float32), pltpu.VMEM((NQ, 2, TQ, 1), jnp.float32)],
        compiler_params=pltpu.CompilerParams(
            dimension_semantics=("parallel", "parallel"), vmem_limit_bytes=VMEM_LIMIT),
        name="attn_stick",
    )(proj, proj, proj)


def _post_kernel(x_ref, ma_ref, mb_ref, p_ref, ge_ref, be_ref, wo_ref, g1_ref, b1_ref,
                 wg_ref, wu_ref, wd_ref, g2_ref, b2_ref, wpg_ref, bpg_ref, wpp_ref, g3_ref, b3_ref, o_ref):
    f32, bf16 = jnp.float32, jnp.bfloat16
    n_sub = TM_POST // SUB_ROWS
    n_chunks = D_FF // FF_CHUNK
    st = [dict() for _ in range(n_sub)]

    def mix(r):
        rows = slice(r * SUB_ROWS, (r + 1) * SUB_ROWS)
        h = _ln(x_ref[rows, :], ge_ref[...], be_ref[...])
        y = (jnp.dot(ma_ref[rows, :], wo_ref[0:SECTION, :], preferred_element_type=f32)
             + jnp.dot(mb_ref[rows, :], wo_ref[SECTION:2 * SECTION, :], preferred_element_type=f32))
        h = _ln(DEEPNORM_ALPHA * h + y, g1_ref[...], b1_ref[...])
        st[r]["h"] = h
        st[r]["hb"] = h.astype(bf16)

    def ffn(r, c):
        cols = slice(c * FF_CHUNK, (c + 1) * FF_CHUNK)
        hb = st[r]["hb"]
        gate = jnp.dot(hb, wg_ref[:, cols], preferred_element_type=f32)
        up = jnp.dot(hb, wu_ref[:, cols], preferred_element_type=f32)
        act = (gate * jax.nn.sigmoid(gate) * up).astype(bf16)
        d = jnp.dot(act, wd_ref[cols, :], preferred_element_type=f32)
        st[r]["f"] = d if c == 0 else st[r]["f"] + d

    def ple(r):
        rows = slice(r * SUB_ROWS, (r + 1) * SUB_ROWS)
        s = st[r]
        s.pop("hb")
        h = _ln(DEEPNORM_ALPHA * s.pop("h") + s.pop("f"), g2_ref[...], b2_ref[...])
        gate = jax.nn.sigmoid(jnp.dot(h.astype(bf16), wpg_ref[...], preferred_element_type=f32) + bpg_ref[...])
        e = jnp.dot(p_ref[rows, :].astype(bf16), wpp_ref[...], preferred_element_type=f32)
        o_ref[rows, :] = _ln(DEEPNORM_ALPHA * h + gate * e, g3_ref[...], b3_ref[...])

    for r in range(n_sub + 1):
        ffn_prev = [(ffn, (r - 1, c)) for c in range(n_chunks)] if r >= 1 else []
        norms = ([(ple, (r - 2,))] if r >= 2 else []) + ([(mix, (r,))] if r < n_sub else [])
        for fn, args in _merge(*[l for l in (ffn_prev, norms) if l]):
            fn(*args)
    ple(n_sub - 1)


def _post(x2d, mix_a, mix_b, p2d, vecs, mats):
    n = x2d.shape[0]
    ge, be, g1, b1, g2, b2, bpg, g3, b3 = vecs
    wo, wg, wu, wd, wpg, wpp = mats
    row = lambda w: pl.BlockSpec((TM_POST, w), lambda i: (i, 0))
    vec = _resident((1, D_MODEL))
    return pl.pallas_call(
        _post_kernel,
        out_shape=jax.ShapeDtypeStruct((n, D_MODEL), jnp.float32),
        grid=(n // TM_POST,),
        in_specs=[row(D_MODEL), row(SECTION), row(SECTION), row(PLE_DIM), vec, vec,
                  _resident(wo.shape), vec, vec,
                  _resident(wg.shape), _resident(wu.shape), _resident(wd.shape), vec, vec,
                  _resident(wpg.shape), vec, _resident(wpp.shape), vec, vec],
        out_specs=row(D_MODEL),
        compiler_params=pltpu.CompilerParams(
            dimension_semantics=("parallel",), vmem_limit_bytes=VMEM_LIMIT_POST),
        name="post",
    )(x2d, mix_a, mix_b, p2d, ge, be, wo, g1, b1, wg, wu, wd, g2, b2, wpg, bpg, wpp, g3, b3)


def kernel(x, p, ln_emb_g, ln_emb_b, w_in, lam_q1, lam_k1, lam_q2, lam_k2, subln_g, w_out, ln1_g, ln1_b,
           w_ffn_gate, w_ffn_up, w_ffn_down, ln2_g, ln2_b, w_ple_gate, b_ple_gate, w_ple_proj, ln3_g, ln3_b):
    bsz, seq, d = x.shape
    assert (seq, d) == (SEQ, D_MODEL) and w_in.shape == (DEPTH, D_MODEL, IN_WIDTH)
    bf16 = jnp.bfloat16
    row = lambda v: v.reshape(1, -1)
    x2d = x.reshape(bsz * seq, d)

    proj = _in_proj(x2d, row(ln_emb_g), row(ln_emb_b), w_in[0].astype(bf16), _rope_tables())
    proj = proj.reshape(bsz, seq, IN_WIDTH)
    lam_params = jnp.stack([lam_q1[0], lam_k1[0], lam_q2[0], lam_k2[0]])
    mix_a = _attn_diff(proj, lam_params, row(subln_g[0]))
    mix_b = _attn_stick(proj)

    vecs = (row(ln_emb_g), row(ln_emb_b), ln1_g, ln1_b, ln2_g, ln2_b, b_ple_gate, ln3_g, ln3_b)
    mats = tuple(w[0].astype(bf16) for w in (w_out, w_ffn_gate, w_ffn_up, w_ffn_down, w_ple_gate, w_ple_proj))
    out = _post(x2d, mix_a.reshape(bsz * seq, SECTION), mix_b.reshape(bsz * seq, SECTION),
                p[0].reshape(bsz * seq, PLE_DIM), vecs, mats)
    return out.reshape(bsz, seq, d)
```

```python
import functools
import math

import jax
import jax.numpy as jnp
from jax import lax
from jax.experimental import pallas as pl
from jax.experimental.pallas import tpu as pltpu

D_MODEL = 1024
SEQ = 2048
DEPTH = 1
CHUNK = 64
HEAD_DIM = 64
A_HEADS = 4
B_HEADS = 8
SECTION = 512
IN_WIDTH = 6 * SECTION
D_FF = 2816
PLE_DIM = 256
ROPE_THETA = 10000.0
LN_EPS = 1e-5
DEEPNORM_ALPHA = (2.0 * DEPTH) ** 0.25
LAMBDA_INIT = 0.8 - 0.6 * math.exp(-0.3 * 0)
Q_SCALE = math.log2(math.e) / math.sqrt(HEAD_DIM)

LANES = 128
VMEM_LIMIT = 56 * 1024 * 1024
VMEM_LIMIT_POST = 60 * 1024 * 1024

TM_PROJ = 1024
TM_POST = 1024
SUB_ROWS = 512
FF_CHUNK = 256
TQ = 256
TK = 256
NQ = SEQ // TQ

_NT = (((1,), (1,)), ((), ()))


def _ln(x, g, b):
    mu = jnp.mean(x, axis=-1, keepdims=True)
    xc = x - mu
    var = jnp.mean(xc * xc, axis=-1, keepdims=True)
    return xc * lax.rsqrt(var + LN_EPS) * g + b


def _resident(shape):
    return pl.BlockSpec(shape, lambda *_: (0,) * len(shape), pipeline_mode=pl.Buffered(1))


def _split_lanes(q):
    lane = lax.broadcasted_iota(jnp.int32, (1, LANES), 1)
    zero = jnp.zeros_like(q)
    return jnp.where(lane < HEAD_DIM, q, zero), jnp.where(lane >= HEAD_DIM, q, zero)


def _merge(*lists):
    keyed = [((i + 0.5) / len(l), li, i, item) for li, l in enumerate(lists) for i, item in enumerate(l)]
    keyed.sort(key=lambda t: t[:3])
    return [t[3] for t in keyed]


def _in_proj_kernel(x_ref, g_ref, b_ref, w_ref, rope_ref, o_ref, h_ref):
    n_sub = TM_PROJ // SUB_ROWS
    hb = {}

    def norm(r):
        rows = slice(r * SUB_ROWS, (r + 1) * SUB_ROWS)
        h = _ln(x_ref[rows, :], g_ref[...], b_ref[...])
        h_ref[rows, :] = h
        hb[r] = h.astype(jnp.bfloat16)

    def project(r, sec):
        rows = slice(r * SUB_ROWS, (r + 1) * SUB_ROWS)
        acc = jnp.dot(hb[r], w_ref[:, sec * SECTION:(sec + 1) * SECTION], preferred_element_type=jnp.float32)
        if sec < 2:
            cos = rope_ref[3 * sec + 0, rows, :]
            sin_lo = rope_ref[3 * sec + 1, rows, :]
            sin_hi = rope_ref[3 * sec + 2, rows, :]
            for c in range(SECTION // LANES):
                cols = slice(sec * SECTION + c * LANES, sec * SECTION + (c + 1) * LANES)
                t = acc[:, c * LANES:(c + 1) * LANES]
                rot = (t * cos + pltpu.roll(t, LANES - HEAD_DIM // 2, 1) * sin_lo
                       + pltpu.roll(t, HEAD_DIM // 2, 1) * sin_hi)
                o_ref[rows, cols] = rot.astype(o_ref.dtype)
        else:
            if sec == 3:
                acc = acc * Q_SCALE
            o_ref[rows, sec * SECTION:(sec + 1) * SECTION] = acc.astype(o_ref.dtype)

    norm(0)
    for r in range(n_sub):
        lanes = [[(project, (r, sec)) for sec in range(6)]]
        if r + 1 < n_sub:
            lanes.append([(norm, (r + 1,))])
        for fn, args in _merge(*lanes):
            fn(*args)


def _rope_tables():
    half = HEAD_DIM // 2
    inv_freq = 1.0 / (ROPE_THETA ** (jnp.arange(0, HEAD_DIM, 2, dtype=jnp.float32) / HEAD_DIM))
    ang = jnp.arange(SEQ, dtype=jnp.float32)[:, None] * inv_freq[None, :]
    ang = jnp.concatenate([ang, ang], axis=-1)
    cos, sin = jnp.cos(ang), jnp.sin(ang)
    first = (jnp.arange(HEAD_DIM) < half)[None, :]
    sin_lo = jnp.where(first, -sin, 0.0)
    sin_hi = jnp.where(first, 0.0, sin)
    k_tabs = jnp.stack([jnp.tile(t, (1, LANES // HEAD_DIM)) for t in (cos, sin_lo, sin_hi)])
    return jnp.concatenate([k_tabs * Q_SCALE, k_tabs], axis=0)


def _in_proj(x2d, g, b, w_bf16, rope):
    n = x2d.shape[0]
    seq_tiles = SEQ // TM_PROJ
    return pl.pallas_call(
        _in_proj_kernel,
        out_shape=(jax.ShapeDtypeStruct((n, IN_WIDTH), jnp.bfloat16),
                   jax.ShapeDtypeStruct((n, D_MODEL), jnp.float32)),
        grid=(n // TM_PROJ,),
        in_specs=[
            pl.BlockSpec((TM_PROJ, D_MODEL), lambda i: (i, 0)),
            _resident((1, D_MODEL)),
            _resident((1, D_MODEL)),
            _resident((D_MODEL, IN_WIDTH)),
            pl.BlockSpec((6, TM_PROJ, LANES), lambda i: (0, i % seq_tiles, 0)),
        ],
        out_specs=(pl.BlockSpec((TM_PROJ, IN_WIDTH), lambda i: (i, 0)),
                   pl.BlockSpec((TM_PROJ, D_MODEL), lambda i: (i, 0))),
        compiler_params=pltpu.CompilerParams(
            dimension_semantics=("parallel",), vmem_limit_bytes=VMEM_LIMIT),
        name="in_proj",
    )(x2d, g, b, w_bf16, rope)


DIFF_ORDER = (0, 2, 4, 6, 7, 5, 3, 1)


def _attn_diff_kernel(lam_ref, q_ref, k_ref, v_ref, g_ref, o_ref):
    f32, bf16 = jnp.float32, jnp.bfloat16
    lp = lam_ref[...]
    lam = (jnp.exp(jnp.sum(lp[0:1] * lp[1:2], axis=-1, keepdims=True))
           - jnp.exp(jnp.sum(lp[2:3] * lp[3:4], axis=-1, keepdims=True)) + LAMBDA_INIT)
    visible = (lax.broadcasted_iota(jnp.int32, (TQ, TK), 0) // CHUNK
               >= lax.broadcasted_iota(jnp.int32, (TQ, TK), 1) // CHUNK)
    maps = range(2)
    qm, s, mx, p, psum, ratio, inv_l1, w = {}, {}, {}, {}, {}, {}, {}, {}

    def score(qi, j):
        if qi not in qm:
            qm[qi] = jnp.concatenate(_split_lanes(q_ref[qi * TQ:(qi + 1) * TQ, :]), axis=0)
        k = k_ref[j * TK:(j + 1) * TK, :]
        both = lax.dot_general(qm[qi], k, _NT, preferred_element_type=f32)
        for m in maps:
            sj = both[m * TQ:(m + 1) * TQ, :]
            if j == qi:
                sj = jnp.where(visible, sj, -jnp.inf)
            s[qi, j, m] = sj

    def rowmax(qi):
        for m in maps:
            blk = functools.reduce(jnp.maximum, [s[qi, j, m] for j in range(qi + 1)])
            mx[qi, m] = jnp.max(blk, axis=-1, keepdims=True)

    def prob(qi, j):
        for m in maps:
            pj = jnp.exp2(s.pop((qi, j, m)) - mx[qi, m])
            p[qi, j, m] = pj
            psum[qi, m] = pj if j == 0 else psum[qi, m] + pj

    def rowsum(qi):
        l1 = jnp.sum(psum.pop((qi, 0)), axis=-1, keepdims=True)
        l2 = jnp.sum(psum.pop((qi, 1)), axis=-1, keepdims=True)
        inv_l1[qi] = 1.0 / l1
        ratio[qi] = lam * l1 / l2

    def weight(qi, j):
        w[qi, j] = (p.pop((qi, j, 0)) - ratio[qi] * p.pop((qi, j, 1))).astype(bf16)

    def out(qi):
        wp = jnp.concatenate([w.pop((qi, j)) for j in range(qi + 1)], axis=1)
        o = jnp.dot(wp, v_ref[0:(qi + 1) * TK, :], preferred_element_type=f32) * inv_l1.pop(qi)
        y = o * lax.rsqrt(jnp.mean(o * o, axis=-1, keepdims=True) + LN_EPS) * g_ref[...]
        o_ref[qi * TQ:(qi + 1) * TQ, :] = (y * (1.0 - LAMBDA_INIT)).astype(o_ref.dtype)

    order = DIFF_ORDER
    for t in range(NQ + 2):
        scoring = [(score, (order[t], j)) for j in range(order[t] + 1)] if t < NQ else []
        softmax = ([(rowmax, (order[t - 1],))] + [(prob, (order[t - 1], j)) for j in range(order[t - 1] + 1)]
                   + [(rowsum, (order[t - 1],))]) if 1 <= t <= NQ else []
        mixing = ([(weight, (order[t - 2], j)) for j in range(order[t - 2] + 1)]
                  + [(out, (order[t - 2],))]) if t >= 2 else []
        for fn, args in _merge(*[l for l in (scoring, softmax, mixing) if l]):
            fn(*args)


def _attn_diff(proj, lam_params, subln_g):
    bsz = proj.shape[0]
    sec_blocks = SECTION // LANES
    return pl.pallas_call(
        _attn_diff_kernel,
        out_shape=jax.ShapeDtypeStruct((bsz, SEQ, A_HEADS * LANES), jnp.bfloat16),
        grid=(bsz, A_HEADS),
        in_specs=[
            _resident((4, HEAD_DIM)),
            pl.BlockSpec((None, SEQ, LANES), lambda b, h: (b, 0, h)),
            pl.BlockSpec((None, SEQ, LANES), lambda b, h: (b, 0, sec_blocks + h)),
            pl.BlockSpec((None, SEQ, LANES), lambda b, h: (b, 0, 2 * sec_blocks + h)),
            _resident((1, LANES)),
        ],
        out_specs=pl.BlockSpec((None, SEQ, LANES), lambda b, h: (b, 0, h)),
        compiler_params=pltpu.CompilerParams(
            dimension_semantics=("parallel", "parallel"), vmem_limit_bytes=VMEM_LIMIT),
        name="attn_diff",
    )(lam_params, proj, proj, proj, subln_g)


STICK_ORDER = (("score", 0), ("suffix", 1), ("logs", 0), ("weight", 1), ("value", 1))
STICK_DEAD = 150.0


def _stick_logs(z):
    mn = jnp.minimum(z, 0.0)
    lg = jnp.log2(1.0 + jnp.exp2(mn + mn - z))
    lb = mn - lg
    return lb, lb - z


def _attn_stick_kernel(q_ref, k_ref, v_ref, o_ref, acc_ref, c_ref):
    f32, bf16 = jnp.float32, jnp.bfloat16
    lane = lax.broadcasted_iota(jnp.int32, (1, LANES), 1)
    causal = (lax.broadcasted_iota(jnp.int32, (TQ, TK), 1)
              < lax.broadcasted_iota(jnp.int32, (TQ, TK), 0))
    later = (lax.broadcasted_iota(jnp.int32, (TK, TK), 0)
             > lax.broadcasted_iota(jnp.int32, (TK, TK), 1)).astype(bf16)
    heads = range(2)

    units = [(qi, j) for qi in range(NQ) for j in range(qi, max(qi - 2, -1), -1)]
    st = {u: {} for u in units}
    qs, run_c, acc = {}, {}, {}

    def score(u):
        qi, j = u
        if qi not in qs:
            qs[qi] = jnp.concatenate(_split_lanes(q_ref[qi * TQ:(qi + 1) * TQ, :]), axis=0)
        k = k_ref[j * TK:(j + 1) * TK, :]
        both = lax.dot_general(qs[qi], k, _NT, preferred_element_type=f32)
        st[u]["z"] = [both[h * TQ:(h + 1) * TQ, :] for h in heads]

    def logs(u):
        qi, j = u
        z = st[u].pop("z")
        terms = [_stick_logs(z[h]) for h in heads]
        ls = [terms[h][1] for h in heads]
        if j == qi:
            ls = [jnp.where(causal, ls[h], 0.0) for h in heads]
        st[u]["lsb"] = [ls[h].astype(bf16) for h in heads]
        st[u]["rs"] = [jnp.sum(ls[h], axis=-1, keepdims=True) for h in heads]
        st[u]["lb"] = [terms[h][0] for h in heads]

    def suffix(u):
        lsb = st[u].pop("lsb")
        both = jnp.dot(jnp.concatenate(lsb, axis=0), later, preferred_element_type=f32)
        st[u]["suf"] = [both[h * TQ:(h + 1) * TQ, :] for h in heads]

    def weight(u):
        qi, j = u
        lb, suf, rs = st[u].pop("lb"), st[u].pop("suf"), st[u].pop("rs")
        a = []
        for h in heads:
            t = lb[h] + suf[h]
            if j == qi:
                ah = jnp.where(causal, jnp.exp2(t), 0.0)
                run_c[qi, h] = rs[h]
            else:
                ah = jnp.exp2(t + run_c[qi, h])
                run_c[qi, h] = run_c[qi, h] + rs[h]
            a.append(ah.astype(bf16))
        st[u]["a"] = a

    def value(u):
        qi, j = u
        a = st[u].pop("a")
        v = v_ref[j * TK:(j + 1) * TK, :]
        both = jnp.dot(jnp.concatenate(a, axis=0), v, preferred_element_type=f32)
        for h in heads:
            pv = both[h * TQ:(h + 1) * TQ, :]
            acc[qi, h] = pv if j == qi else acc[qi, h] + pv
        if j == max(qi - 1, 0):
            for h in heads:
                acc_ref[qi, h] = acc.pop((qi, h))
                c_ref[qi, h] = run_c[qi, h]

    stages = dict(score=score, logs=logs, suffix=suffix, weight=weight, value=value)
    for t in range(len(units) + max(d for _, d in STICK_ORDER)):
        for name, delay in STICK_ORDER:
            if 0 <= t - delay < len(units):
                stages[name](units[t - delay])

    def tail_step(qi, i, carry):
        j = qi - 2 - i
        rows = pl.ds(pl.multiple_of(j * TK, TK), TK)
        k, v = k_ref[rows, :], v_ref[rows, :]
        q_pair = _split_lanes(q_ref[qi * TQ:(qi + 1) * TQ, :])
        out = []
        for h in heads:
            c, acc_h = carry[h]
            lb, ls = _stick_logs(lax.dot_general(q_pair[h], k, _NT, preferred_element_type=f32))
            suf = jnp.dot(ls.astype(bf16), later, preferred_element_type=f32)
            a = jnp.exp2(lb + suf + c)
            acc_h = acc_h + jnp.dot(a.astype(bf16), v, preferred_element_type=f32)
            out.append((c + jnp.sum(ls, axis=-1, keepdims=True), acc_h))
        return tuple(out)

    def remainder(qi):
        init = tuple((c_ref[qi, h], acc_ref[qi, h]) for h in heads)
        done = lax.fori_loop(0, qi - 1, functools.partial(tail_step, qi), init)
        for h in heads:
            acc_ref[qi, h] = done[h][1]

    least_dead = functools.reduce(jnp.maximum, [run_c[qi, h] for qi in range(2, NQ) for h in heads])

    @pl.when(jnp.max(least_dead) > -STICK_DEAD)
    def _():
        for qi in range(2, NQ):
            pl.when(jnp.max(jnp.maximum(c_ref[qi, 0], c_ref[qi, 1])) > -STICK_DEAD)(
                functools.partial(remainder, qi))

    for qi in range(NQ):
        o_ref[qi * TQ:(qi + 1) * TQ, :] = jnp.where(
            lane < HEAD_DIM, acc_ref[qi, 0], acc_ref[qi, 1]).astype(o_ref.dtype)


def _attn_stick(proj):
    bsz = proj.shape[0]
    sec_blocks = SECTION // LANES
    return pl.pallas_call(
        _attn_stick_kernel,
        out_shape=jax.ShapeDtypeStruct((bsz, SEQ, SECTION), jnp.bfloat16),
        grid=(bsz, sec_blocks),
        in_specs=[
            pl.BlockSpec((None, SEQ, LANES), lambda b, h: (b, 0, 3 * sec_blocks + h)),
            pl.BlockSpec((None, SEQ, LANES), lambda b, h: (b, 0, 4 * sec_blocks + h)),
            pl.BlockSpec((None, SEQ, LANES), lambda b, h: (b, 0, 5 * sec_blocks + h)),
        ],
        out_specs=pl.BlockSpec((None, SEQ, LANES), lambda b, h: (b, 0, h)),
        scratch_shapes=[pltpu.VMEM((NQ, 2, TQ, LANES), jnp.float32), pltpu.VMEM((NQ, 2, TQ, 1), jnp.float32)],
        compiler_params=pltpu.CompilerParams(
            dimension_semantics=("parallel", "parallel"), vmem_limit_bytes=VMEM_LIMIT),
        name="attn_stick",
    )(proj, proj, proj)


def _post_kernel(h_ref, ma_ref, mb_ref, p_ref, wo_ref, g1_ref, b1_ref,
                 wg_ref, wu_ref, wd_ref, g2_ref, b2_ref, wpg_ref, bpg_ref, wpp_ref, g3_ref, b3_ref, o_ref):
    f32, bf16 = jnp.float32, jnp.bfloat16
    n_sub = TM_POST // SUB_ROWS
    n_chunks = D_FF // FF_CHUNK
    st = [dict() for _ in range(n_sub)]

    def mix(r):
        rows = slice(r * SUB_ROWS, (r + 1) * SUB_ROWS)
        h = h_ref[rows, :]
        y = (jnp.dot(ma_ref[rows, :], wo_ref[0:SECTION, :], preferred_element_type=f32)
             + jnp.dot(mb_ref[rows, :], wo_ref[SECTION:2 * SECTION, :], preferred_element_type=f32))
        h = _ln(DEEPNORM_ALPHA * h + y, g1_ref[...], b1_ref[...])
        st[r]["h"] = h
        st[r]["hb"] = h.astype(bf16)

    def ffn(r, c):
        cols = slice(c * FF_CHUNK, (c + 1) * FF_CHUNK)
        hb = st[r]["hb"]
        gate = jnp.dot(hb, wg_ref[:, cols], preferred_element_type=f32)
        up = jnp.dot(hb, wu_ref[:, cols], preferred_element_type=f32)
        act = (gate * jax.nn.sigmoid(gate) * up).astype(bf16)
        d = jnp.dot(act, wd_ref[cols, :], preferred_element_type=f32)
        st[r]["f"] = d if c == 0 else st[r]["f"] + d

    def ple(r):
        rows = slice(r * SUB_ROWS, (r + 1) * SUB_ROWS)
        s = st[r]
        s.pop("hb")
        h = _ln(DEEPNORM_ALPHA * s.pop("h") + s.pop("f"), g2_ref[...], b2_ref[...])
        gate = jax.nn.sigmoid(jnp.dot(h.astype(bf16), wpg_ref[...], preferred_element_type=f32) + bpg_ref[...])
        e = jnp.dot(p_ref[rows, :].astype(bf16), wpp_ref[...], preferred_element_type=f32)
        o_ref[rows, :] = _ln(DEEPNORM_ALPHA * h + gate * e, g3_ref[...], b3_ref[...])

    for r in range(n_sub + 1):
        ffn_prev = [(ffn, (r - 1, c)) for c in range(n_chunks)] if r >= 1 else []
        norms = ([(ple, (r - 2,))] if r >= 2 else []) + ([(mix, (r,))] if r < n_sub else [])
        for fn, args in _merge(*[l for l in (ffn_prev, norms) if l]):
            fn(*args)
    ple(n_sub - 1)


def _post(h2d, mix_a, mix_b, p2d, vecs, mats):
    n = h2d.shape[0]
    g1, b1, g2, b2, bpg, g3, b3 = vecs
    wo, wg, wu, wd, wpg, wpp = mats
    row = lambda w: pl.BlockSpec((TM_POST, w), lambda i: (i, 0))
    vec = _resident((1, D_MODEL))
    return pl.pallas_call(
        _post_kernel,
        out_shape=jax.ShapeDtypeStruct((n, D_MODEL), jnp.float32),
        grid=(n // TM_POST,),
        in_specs=[row(D_MODEL), row(SECTION), row(SECTION), row(PLE_DIM),
                  _resident(wo.shape), vec, vec,
                  _resident(wg.shape), _resident(wu.shape), _resident(wd.shape), vec, vec,
                  _resident(wpg.shape), vec, _resident(wpp.shape), vec, vec],
        out_specs=row(D_MODEL),
        compiler_params=pltpu.CompilerParams(
            dimension_semantics=("parallel",), vmem_limit_bytes=VMEM_LIMIT_POST),
        name="post",
    )(h2d, mix_a, mix_b, p2d, wo, g1, b1, wg, wu, wd, g2, b2, wpg, bpg, wpp, g3, b3)


def kernel(x, p, ln_emb_g, ln_emb_b, w_in, lam_q1, lam_k1, lam_q2, lam_k2, subln_g, w_out, ln1_g, ln1_b,
           w_ffn_gate, w_ffn_up, w_ffn_down, ln2_g, ln2_b, w_ple_gate, b_ple_gate, w_ple_proj, ln3_g, ln3_b):
    bsz, seq, d = x.shape
    assert (seq, d) == (SEQ, D_MODEL) and w_in.shape == (DEPTH, D_MODEL, IN_WIDTH)
    bf16 = jnp.bfloat16
    row = lambda v: v.reshape(1, -1)
    x2d = x.reshape(bsz * seq, d)

    proj, h2d = _in_proj(x2d, row(ln_emb_g), row(ln_emb_b), w_in[0].astype(bf16), _rope_tables())
    proj = proj.reshape(bsz, seq, IN_WIDTH)
    lam_params = jnp.stack([lam_q1[0], lam_k1[0], lam_q2[0], lam_k2[0]])
    mix_a = _attn_diff(proj, lam_params, row(subln_g[0]))
    mix_b = _attn_stick(proj)

    vecs = (ln1_g, ln1_b, ln2_g, ln2_b, b_ple_gate, ln3_g, ln3_b)
    mats = tuple(w[0].astype(bf16) for w in (w_out, w_ffn_gate, w_ffn_up, w_ffn_down, w_ple_gate, w_ple_proj))
    out = _post(h2d, mix_a.reshape(bsz * seq, SECTION), mix_b.reshape(bsz * seq, SECTION),
                p[0].reshape(bsz * seq, PLE_DIM), vecs, mats)
    return out.reshape(bsz, seq, d)
```
